```python
import math
import jax, jax.numpy as jnp
from jax import lax
import numpy as np

D_MODEL = 1024
BATCH = 4
SEQ = 4096
DEPTH = 2

N_MEM = 256
CONV_WIDTH_CH = D_MODEL // 2
CONV_K = 3
HEAD_DIM = 64
N_Q_HEADS = (D_MODEL // 2) // HEAD_DIM
N_KV_HEADS = 2
GROUP = N_Q_HEADS // N_KV_HEADS
ATTN_WIDTH = N_Q_HEADS * HEAD_DIM
KV_WIDTH = N_KV_HEADS * HEAD_DIM
WINDOW = 128
BLOCK = WINDOW
ROPE_THETA = 10000.0
N_X_HEADS = 4
X_HEAD_DIM = D_MODEL // N_X_HEADS
D_FF = 4 * D_MODEL
EPS = 1e-6
IN_COLS = 3 * CONV_WIDTH_CH + ATTN_WIDTH + 2 * KV_WIDTH

kernel_name = "hybrid_conv_swa_sink_xattn_block"


def rmsnorm(x, g):
    xf = x.astype(jnp.float32)
    y = xf * lax.rsqrt(jnp.mean(xf * xf, axis=-1, keepdims=True) + EPS)
    return (y * g.astype(jnp.float32)).astype(x.dtype)


def rotary_tables(positions, dtype):
    inv_freq = ROPE_THETA ** (-jnp.arange(0, HEAD_DIM, 2, dtype=jnp.float32) / HEAD_DIM)
    ang = positions.astype(jnp.float32)[..., None] * inv_freq
    return jnp.cos(ang)[:, :, None, :].astype(dtype), jnp.sin(ang)[:, :, None, :].astype(dtype)


def apply_rotary(t, cos, sin):
    t1, t2 = jnp.split(t, 2, axis=-1)
    return jnp.concatenate([t1 * cos - t2 * sin, t2 * cos + t1 * sin], axis=-1)


def causal_short_conv(u, w):
    s = u.shape[1]
    up = jnp.pad(u, ((0, 0), (CONV_K - 1, 0), (0, 0)))
    return sum(w[k] * up[:, k:k + s] for k in range(CONV_K))


def sliding_window_attention(q, k, v, sinks):
    b, s = q.shape[0], q.shape[1]
    nb = s // BLOCK
    qb = q.reshape(b, nb, BLOCK, N_KV_HEADS, GROUP, HEAD_DIM)
    kb = k.reshape(b, nb, BLOCK, N_KV_HEADS, HEAD_DIM)
    vb = v.reshape(b, nb, BLOCK, N_KV_HEADS, HEAD_DIM)

    def with_prev(t):
        prev = jnp.pad(t[:, :-1], ((0, 0), (1, 0), (0, 0), (0, 0), (0, 0)))
        return jnp.concatenate([prev, t], axis=2)

    kw, vw = with_prev(kb), with_prev(vb)
    scale = 1.0 / math.sqrt(HEAD_DIM)
    sc = jnp.einsum('bnqhgd,bnkhd->bnhgqk', qb, kw).astype(jnp.float32) * scale
    qi = jnp.arange(BLOCK)[:, None]
    ki = jnp.arange(2 * BLOCK)[None, :]
    diff = BLOCK + qi - ki
    band = (diff >= 0) & (diff < WINDOW)
    blk = jnp.arange(nb)[:, None, None]
    mask = band[None] & ((blk > 0) | (ki >= BLOCK)[None])
    sc = jnp.where(mask[None, :, None, None], sc, -jnp.inf)
    sink = sinks.astype(jnp.float32).reshape(N_KV_HEADS, GROUP)[None, None, :, :, None, None]
    m = jnp.maximum(jnp.max(sc, axis=-1, keepdims=True), sink)
    p = jnp.exp(sc - m)
    denom = jnp.sum(p, axis=-1, keepdims=True) + jnp.exp(sink - m)
    p = (p / denom).astype(v.dtype)
    o = jnp.einsum('bnhgqk,bnkhd->bnqhgd', p, vw)
    return o.reshape(b, s, ATTN_WIDTH)


def memory_cross_attention(h, memn, wq, wkv, wo):
    b, s, _ = h.shape
    q = (h @ wq).reshape(b, s, N_X_HEADS, X_HEAD_DIM)
    kv = memn @ wkv
    k, v = jnp.split(kv, 2, axis=-1)
    k = k.reshape(b, -1, N_X_HEADS, X_HEAD_DIM)
    v = v.reshape(b, -1, N_X_HEADS, X_HEAD_DIM)
    sc = jnp.einsum('bshd,bmhd->bhsm', q, k).astype(jnp.float32) / math.sqrt(X_HEAD_DIM)
    p = jax.nn.softmax(sc, axis=-1).astype(v.dtype)
    o = jnp.einsum('bhsm,bmhd->bshd', p, v).reshape(b, s, D_MODEL)
    return o @ wo


def setup_inputs(seed: int = 0) -> dict:
    key = jax.random.key(seed)
    ks = jax.random.split(key, 20)
    f32 = jnp.float32

    def nrm(k, shape, scale):
        return jax.random.normal(k, shape, f32) * scale

    def gain(k, shape):
        return 1.0 + 0.02 * jax.random.normal(k, shape, f32)

    return {
        "x": jax.random.normal(ks[0], (BATCH, SEQ, D_MODEL), f32),
        "mem": jax.random.normal(ks[1], (BATCH, N_MEM, D_MODEL), f32),
        "positions": jnp.broadcast_to(jnp.arange(SEQ, dtype=jnp.int32), (BATCH, SEQ)),
        "norm_mix_g": gain(ks[2], (DEPTH, D_MODEL)),
        "w_in": nrm(ks[3], (DEPTH, D_MODEL, IN_COLS), D_MODEL ** -0.5),
        "conv_w": nrm(ks[4], (DEPTH, CONV_K, CONV_WIDTH_CH), CONV_K ** -0.5),
        "sinks": nrm(ks[5], (DEPTH, N_Q_HEADS), 0.5),
        "gnorm_conv_g": gain(ks[6], (DEPTH, CONV_WIDTH_CH)),
        "gnorm_attn_g": gain(ks[7], (DEPTH, ATTN_WIDTH)),
        "w_out": nrm(ks[8], (DEPTH, CONV_WIDTH_CH + ATTN_WIDTH, D_MODEL), (CONV_WIDTH_CH + ATTN_WIDTH) ** -0.5),
        "norm_x_g": gain(ks[9], (DEPTH, D_MODEL)),
        "norm_mem_g": gain(ks[10], (DEPTH, D_MODEL)),
        "wx_q": nrm(ks[11], (DEPTH, D_MODEL, D_MODEL), D_MODEL ** -0.5),
        "wx_kv": nrm(ks[12], (DEPTH, D_MODEL, 2 * D_MODEL), D_MODEL ** -0.5),
        "wx_o": nrm(ks[13], (DEPTH, D_MODEL, D_MODEL), D_MODEL ** -0.5),
        "norm_mlp_g": gain(ks[14], (DEPTH, D_MODEL)),
        "w_up": nrm(ks[15], (DEPTH, D_MODEL, D_FF), D_MODEL ** -0.5),
        "w_down": nrm(ks[16], (DEPTH, D_FF, D_MODEL), D_FF ** -0.5),
        "final_g": gain(ks[17], (D_MODEL,)),
    }


def reference(x, mem, positions, norm_mix_g, w_in, conv_w, sinks, gnorm_conv_g,
              gnorm_attn_g, w_out, norm_x_g, norm_mem_g, wx_q, wx_kv, wx_o,
              norm_mlp_g, w_up, w_down, final_g):
    b, s, _ = x.shape
    cos, sin = rotary_tables(positions, x.dtype)
    splits = np.cumsum([CONV_WIDTH_CH, CONV_WIDTH_CH, CONV_WIDTH_CH, ATTN_WIDTH, KV_WIDTH]).tolist()
    for l in range(DEPTH):
        h = rmsnorm(x, norm_mix_g[l])
        u = h @ w_in[l]
        gb, gc, xc, q, k, v = jnp.split(u, splits, axis=-1)
        conv_out = gb * causal_short_conv(gc * xc, conv_w[l])
        q = apply_rotary(q.reshape(b, s, N_Q_HEADS, HEAD_DIM), cos, sin)
        k = apply_rotary(k.reshape(b, s, N_KV_HEADS, HEAD_DIM), cos, sin)
        v = v.reshape(b, s, N_KV_HEADS, HEAD_DIM)
        attn_out = sliding_window_attention(q, k, v, sinks[l])
        mixed = jnp.concatenate([rmsnorm(conv_out, gnorm_conv_g[l]),
                                 rmsnorm(attn_out, gnorm_attn_g[l])], axis=-1)
        x = x + mixed @ w_out[l]
        x = x + memory_cross_attention(rmsnorm(x, norm_x_g[l]), rmsnorm(mem, norm_mem_g[l]),
                                       wx_q[l], wx_kv[l], wx_o[l])
        hm = rmsnorm(x, norm_mlp_g[l])
        x = x + jnp.square(jax.nn.relu(hm @ w_up[l])) @ w_down[l]
    return rmsnorm(x, final_g)
```

```python
import functools
import math

import jax
import jax.numpy as jnp
from jax import lax
from jax.experimental import pallas as pl
from jax.experimental.pallas import tpu as pltpu

D_MODEL = 1024
N_MEM = 256
CONV_CH = 512
CONV_K = 3
HEAD_DIM = 64
N_Q_HEADS = 8
N_KV_HEADS = 2
ATTN_W = N_Q_HEADS * HEAD_DIM
KV_W = N_KV_HEADS * HEAD_DIM
WINDOW = 128
ROPE_THETA = 10000.0
N_X_HEADS = 4
X_HEAD_DIM = D_MODEL // N_X_HEADS
D_FF = 4 * D_MODEL
EPS = 1e-6
IN_COLS = 3 * CONV_CH + ATTN_W + 2 * KV_W
Q_OFF = 3 * CONV_CH
K_OFF = Q_OFF + ATTN_W
V_OFF = K_OFF + KV_W

LANES = 128
SUBLANES = 8
VMEM_LIMIT_BYTES = 56 * 1024 * 1024

SEQ_TILE = 512
FF_CHUNK = 1024
CARRY_ROWS = SUBLANES

BF16 = jnp.bfloat16
F32 = jnp.float32


def _rms(x, g):
    ms = jnp.mean(x * x, axis=-1, keepdims=True)
    return x * lax.rsqrt(ms + EPS) * g


def _dot(a, b):
    return jnp.dot(a, b, preferred_element_type=F32)


def _rope_kernel(pos_ref, inv_ref, sign_ref, cos_ref, sin_ref):
    ang = pos_ref[0].astype(F32) * inv_ref[...]
    cos_ref[0] = jnp.cos(ang)
    sin_ref[0] = jnp.sin(ang) * sign_ref[...]


def _rope_tables(positions):
    b, s = positions.shape
    inv_freq = ROPE_THETA ** (-jnp.arange(0, HEAD_DIM, 2, dtype=F32) / HEAD_DIM)
    inv = jnp.tile(inv_freq, LANES // (HEAD_DIM // 2)).reshape(1, LANES)
    half = HEAD_DIM // 2
    sign = jnp.where((jnp.arange(LANES) // half) % 2 == 0, -1.0, 1.0).astype(F32).reshape(1, LANES)
    tile = pl.BlockSpec((1, SEQ_TILE, LANES), lambda i, j: (i, j, 0))
    const = pl.BlockSpec((1, LANES), lambda i, j: (0, 0))
    return pl.pallas_call(
        _rope_kernel,
        grid=(b, s // SEQ_TILE),
        in_specs=[pl.BlockSpec((1, SEQ_TILE, 1), lambda i, j: (i, j, 0)), const, const],
        out_specs=[tile, tile],
        out_shape=[jax.ShapeDtypeStruct((b, s, LANES), F32)] * 2,
        name="rope_tables",
    )(positions.reshape(b, s, 1), inv, sign)


def _mix_kernel(sinks_ref, x_ref, cos_ref, sin_ref, gmix_ref, win_ref, convw_ref,
                gconv_ref, gattn_ref, wout_ref, o_ref,
                ke0, ko0, ke1, ko1, ve0, vo0, ve1, vo1, cbuf, qbuf, abuf):
    ts = SEQ_TILE
    nblk = ts // WINDOW
    kbufs = ((ke0, ko0), (ke1, ko1))
    vbufs = ((ve0, vo0), (ve1, vo1))
    seq_start = pl.program_id(1) == 0

    @pl.when(seq_start)
    def _():
        for buf in (ke0, ko0, ke1, ko1, ve0, vo0, ve1, vo1):
            buf[0:WINDOW, :] = jnp.zeros((WINDOW, LANES), BF16)
        cbuf[0:CARRY_ROWS, :] = jnp.zeros((CARRY_ROWS, CONV_CH), F32)

    x = x_ref[0]
    h = _rms(x, gmix_ref[...]).astype(BF16)

    uc = _dot(h, win_ref[:, 0:Q_OFF])
    cx = uc[:, CONV_CH:2 * CONV_CH] * uc[:, 2 * CONV_CH:3 * CONV_CH]
    cbuf[CARRY_ROWS:CARRY_ROWS + ts, :] = cx
    w = convw_ref[...]
    y = (w[2:3, :] * cx
         + w[1:2, :] * cbuf[CARRY_ROWS - 1:CARRY_ROWS - 1 + ts, :]
         + w[0:1, :] * cbuf[CARRY_ROWS - 2:CARRY_ROWS - 2 + ts, :])
    conv = uc[:, 0:CONV_CH] * y
    cbuf[0:CARRY_ROWS, :] = cx[ts - CARRY_ROWS:ts, :]
    conv_n = _rms(conv, gconv_ref[...]).astype(BF16)

    cosf = cos_ref[0]
    sins = sin_ref[0]
    lane = lax.broadcasted_iota(jnp.int32, (ts, LANES), 1)
    first_half = (lane & (HEAD_DIM // 2)) == 0
    low_head = lane < HEAD_DIM

    def rope(t):
        rot = jnp.where(first_half,
                        pltpu.roll(t, LANES - HEAD_DIM // 2, 1),
                        pltpu.roll(t, HEAD_DIM // 2, 1))
        return t * cosf + rot * sins

    uq = _dot(h, win_ref[:, Q_OFF:K_OFF])
    scale = 1.0 / math.sqrt(HEAD_DIM)
    for c in range(ATTN_W // LANES):
        qc = rope(uq[:, c * LANES:(c + 1) * LANES]) * scale
        qbuf[:, c * LANES:(c + 1) * LANES] = qc.astype(BF16)

    ukv = _dot(h, win_ref[:, K_OFF:IN_COLS])
    k = rope(ukv[:, 0:KV_W])
    v = ukv[:, KV_W:2 * KV_W]
    for src, bufs in ((k, kbufs), (v, vbufs)):
        e0 = jnp.where(low_head, src, 0.0)
        o1 = jnp.where(low_head, 0.0, src)
        bufs[0][0][WINDOW:WINDOW + ts, :] = e0.astype(BF16)
        bufs[0][1][WINDOW:WINDOW + ts, :] = pltpu.roll(e0, HEAD_DIM, 1).astype(BF16)
        bufs[1][0][WINDOW:WINDOW + ts, :] = pltpu.roll(o1, HEAD_DIM, 1).astype(BF16)
        bufs[1][1][WINDOW:WINDOW + ts, :] = o1.astype(BF16)

    rows = 2 * WINDOW
    qi = lax.broadcasted_iota(jnp.int32, (rows, 2 * WINDOW), 0) & (WINDOW - 1)
    ki = lax.broadcasted_iota(jnp.int32, (rows, 2 * WINDOW), 1)
    diff = WINDOW + qi - ki
    band = (diff >= 0) & (diff < WINDOW)
    band_first = band & (ki >= WINDOW * seq_start.astype(jnp.int32))
    top_rows = lax.broadcasted_iota(jnp.int32, (rows, 1), 0) < WINDOW
    out_low = lax.broadcasted_iota(jnp.int32, (rows, LANES), 1) < HEAD_DIM

    for j in range(nblk):
        mask = band_first if j == 0 else band
        r0 = j * WINDOW
        for g in range(N_KV_HEADS):
            qg = jnp.concatenate(
                [qbuf[r0:r0 + WINDOW, (2 * g) * LANES:(2 * g + 1) * LANES],
                 qbuf[r0:r0 + WINDOW, (2 * g + 1) * LANES:(2 * g + 2) * LANES]], axis=0)
            kk = jnp.concatenate([kbufs[g][0][r0:r0 + 2 * WINDOW, :],
                                  kbufs[g][1][r0:r0 + 2 * WINDOW, :]], axis=0)
            vv = jnp.concatenate([vbufs[g][0][r0:r0 + 2 * WINDOW, :],
                                  vbufs[g][1][r0:r0 + 2 * WINDOW, :]], axis=0)
            s = lax.dot_general(qg, kk, (((1,), (1,)), ((), ())),
                                preferred_element_type=F32)
            ps, inv_l = [], []
            for half in range(2):
                sh = jnp.where(mask, s[:, half * 2 * WINDOW:(half + 1) * 2 * WINDOW], -jnp.inf)
                sink = jnp.where(top_rows, sinks_ref[4 * g + half], sinks_ref[4 * g + 2 + half])
                m = jnp.maximum(jnp.max(sh, axis=-1, keepdims=True), sink)
                p = jnp.exp(sh - m)
                l = jnp.sum(p, axis=-1, keepdims=True) + jnp.exp(sink - m)
                ps.append(p.astype(BF16))
                inv_l.append(1.0 / l)
            o = _dot(jnp.concatenate(ps, axis=1), vv)
            o = o * jnp.where(out_low, inv_l[0], inv_l[1])
            abuf[r0:r0 + WINDOW, (2 * g) * LANES:(2 * g + 1) * LANES] = o[0:WINDOW]
            abuf[r0:r0 + WINDOW, (2 * g + 1) * LANES:(2 * g + 2) * LANES] = o[WINDOW:rows]

    for buf in (ke0, ko0, ke1, ko1, ve0, vo0, ve1, vo1):
        buf[0:WINDOW, :] = buf[ts:ts + WINDOW, :]

    attn_n = _rms(abuf[...], gattn_ref[...]).astype(BF16)
    o_ref[0] = (x + _dot(conv_n, wout_ref[0:CONV_CH, :])
                + _dot(attn_n, wout_ref[CONV_CH:CONV_CH + ATTN_W, :]))


def _mix_layer(x, cos, sin, sinks, gmix, w_in, conv_w, gconv, gattn, w_out):
    b, s, d = x.shape
    ts = SEQ_TILE
    tok = lambda w: pl.BlockSpec((1, ts, w), lambda i, j: (i, j, 0))
    full = lambda *shape: pl.BlockSpec(shape, lambda i, j: (0,) * len(shape))
    kv_scratch = [pltpu.VMEM((ts + WINDOW, LANES), BF16)] * 8
    return pl.pallas_call(
        _mix_kernel,
        grid=(b, s // ts),
        in_specs=[pl.BlockSpec(memory_space=pltpu.SMEM),
                  tok(d), tok(LANES), tok(LANES),
                  full(1, d), full(d, IN_COLS), full(CONV_K, CONV_CH),
                  full(1, CONV_CH), full(1, ATTN_W), full(CONV_CH + ATTN_W, d)],
        out_specs=tok(d),
        out_shape=jax.ShapeDtypeStruct((b, s, d), F32),
        scratch_shapes=kv_scratch + [
            pltpu.VMEM((ts + CARRY_ROWS, CONV_CH), F32),
            pltpu.VMEM((ts, ATTN_W), BF16),
            pltpu.VMEM((ts, ATTN_W), F32)],
        compiler_params=pltpu.CompilerParams(
            dimension_semantics=("arbitrary", "arbitrary"),
            vmem_limit_bytes=VMEM_LIMIT_BYTES),
        name="token_mix",
    )(sinks, x, cos, sin, gmix.reshape(1, d), w_in, conv_w,
      gconv.reshape(1, CONV_CH), gattn.reshape(1, ATTN_W), w_out)


def _memkv_kernel(mem_ref, g_ref, wkv_ref, kt_ref, v_ref):
    memn = _rms(mem_ref[0], g_ref[0]).astype(BF16)
    kv = _dot(memn, wkv_ref[0])
    kt_ref[0, 0] = kv[:, 0:D_MODEL].T.astype(BF16)
    v_ref[0, 0] = kv[:, D_MODEL:2 * D_MODEL].astype(BF16)


def _mem_kv(mem, norm_mem_g, wx_kv):
    b, m, d = mem.shape
    depth = wx_kv.shape[0]
    return pl.pallas_call(
        _memkv_kernel,
        grid=(depth, b),
        in_specs=[pl.BlockSpec((1, m, d), lambda l, i: (i, 0, 0)),
                  pl.BlockSpec((1, 1, d), lambda l, i: (l, 0, 0)),
                  pl.BlockSpec((1, d, 2 * d), lambda l, i: (l, 0, 0))],
        out_specs=[pl.BlockSpec((1, 1, d, m), lambda l, i: (l, i, 0, 0)),
                   pl.BlockSpec((1, 1, m, d), lambda l, i: (l, i, 0, 0))],
        out_shape=[jax.ShapeDtypeStruct((depth, b, d, m), BF16),
                   jax.ShapeDtypeStruct((depth, b, m, d), BF16)],
        compiler_params=pltpu.CompilerParams(vmem_limit_bytes=VMEM_LIMIT_BYTES),
        name="mem_kv",
    )(mem, norm_mem_g.reshape(depth, 1, d), wx_kv)


def _xattn_kernel(x_ref, g_ref, wq_ref, kt_ref, v_ref, wo_ref, o_ref):
    x = x_ref[0]
    h = _rms(x, g_ref[...]).astype(BF16)
    scale = 1.0 / math.sqrt(X_HEAD_DIM)
    q = (_dot(h, wq_ref[...]) * scale).astype(BF16)
    outs = []
    for hd in range(N_X_HEADS):
        sl = slice(hd * X_HEAD_DIM, (hd + 1) * X_HEAD_DIM)
        s = _dot(q[:, sl], kt_ref[0, sl, :])
        m = jnp.max(s, axis=-1, keepdims=True)
        p = jnp.exp(s - m)
        l = jnp.sum(p, axis=-1, keepdims=True)
        o = _dot(p.astype(BF16), v_ref[0, :, sl]) * (1.0 / l)
        outs.append(o.astype(BF16))
    o_ref[0] = x + _dot(jnp.concatenate(outs, axis=1), wo_ref[...])


def _xattn_layer(x, g, wq, kt, v, wo):
    b, s, d = x.shape
    ts = SEQ_TILE
    tok = pl.BlockSpec((1, ts, d), lambda i, j: (i, j, 0))
    full = lambda *shape: pl.BlockSpec(shape, lambda i, j: (0,) * len(shape))
    return pl.pallas_call(
        _xattn_kernel,
        grid=(b, s // ts),
        in_specs=[tok, full(1, d), full(d, d),
                  pl.BlockSpec((1, d, N_MEM), lambda i, j: (i, 0, 0)),
                  pl.BlockSpec((1, N_MEM, d), lambda i, j: (i, 0, 0)),
                  full(d, d)],
        out_specs=tok,
        out_shape=jax.ShapeDtypeStruct((b, s, d), F32),
        compiler_params=pltpu.CompilerParams(vmem_limit_bytes=VMEM_LIMIT_BYTES),
        name="mem_xattn",
    )(x, g.reshape(1, d), wq, kt, v, wo)


def _mlp_kernel(x_ref, g_ref, wup_ref, wdn_ref, fg_ref, o_ref, *, final):
    x = x_ref[...]
    h = _rms(x, g_ref[...]).astype(BF16)
    acc = x
    for c in range(D_FF // FF_CHUNK):
        sl = slice(c * FF_CHUNK, (c + 1) * FF_CHUNK)
        up = _dot(h, wup_ref[:, sl])
        act = jnp.square(jnp.maximum(up, 0.0)).astype(BF16)
        acc = acc + _dot(act, wdn_ref[sl, :])
    if final:
        acc = _rms(acc, fg_ref[...])
    o_ref[...] = acc


def _mlp_layer(x2d, g, w_up, w_down, final_g, final):
    n, d = x2d.shape
    tm = SEQ_TILE
    tok = pl.BlockSpec((tm, d), lambda i: (i, 0))
    full = lambda *shape: pl.BlockSpec(shape, lambda i: (0,) * len(shape))
    return pl.pallas_call(
        functools.partial(_mlp_kernel, final=final),
        grid=(n // tm,),
        in_specs=[tok, full(1, d), full(d, D_FF), full(D_FF, d), full(1, d)],
        out_specs=tok,
        out_shape=jax.ShapeDtypeStruct((n, d), F32),
        compiler_params=pltpu.CompilerParams(vmem_limit_bytes=VMEM_LIMIT_BYTES),
        name="relu2_mlp",
    )(x2d, g.reshape(1, d), w_up, w_down, final_g.reshape(1, d))


def kernel(x, mem, positions, norm_mix_g, w_in, conv_w, sinks, gnorm_conv_g, gnorm_attn_g, w_out, norm_x_g, norm_mem_g, wx_q, wx_kv, wx_o, norm_mlp_g, w_up, w_down, final_g):
    b, s, d = x.shape
    depth = w_in.shape[0]
    cos, sin = _rope_tables(positions)
    kt, v = _mem_kv(mem, norm_mem_g, wx_kv.astype(BF16))
    w_in, w_out, wx_q, wx_o, w_up, w_down = (
        t.astype(BF16) for t in (w_in, w_out, wx_q, wx_o, w_up, w_down))
    for l in range(depth):
        x = _mix_layer(x, cos, sin, sinks[l], norm_mix_g[l], w_in[l], conv_w[l],
                       gnorm_conv_g[l], gnorm_attn_g[l], w_out[l])
        x = _xattn_layer(x, norm_x_g[l], wx_q[l], kt[l], v[l], wx_o[l])
        x = _mlp_layer(x.reshape(b * s, d), norm_mlp_g[l], w_up[l], w_down[l],
                       final_g, final=(l == depth - 1)).reshape(b, s, d)
    return x
```

```python
import functools
import math

import jax
import jax.numpy as jnp
from jax import lax
from jax.experimental import pallas as pl
from jax.experimental.pallas import tpu as pltpu

D_MODEL = 1024
N_MEM = 256
CONV_CH = 512
CONV_K = 3
HEAD_DIM = 64
N_Q_HEADS = 8
N_KV_HEADS = 2
ATTN_W = N_Q_HEADS * HEAD_DIM
KV_W = N_KV_HEADS * HEAD_DIM
WINDOW = 128
ROPE_THETA = 10000.0
N_X_HEADS = 4
X_HEAD_DIM = D_MODEL // N_X_HEADS
D_FF = 4 * D_MODEL
EPS = 1e-6
IN_COLS = 3 * CONV_CH + ATTN_W + 2 * KV_W
Q_OFF = 3 * CONV_CH
K_OFF = Q_OFF + ATTN_W

LANES = 128
SUBLANES = 8
VMEM_LIMIT_BYTES = 56 * 1024 * 1024

SEQ_TILE = 512
SUB_TILE = 256
MLP_TILE = 1024
FF_CHUNK = 1024
CARRY_ROWS = SUBLANES
ROPE_PACK = LANES // (HEAD_DIM // 2)

BF16 = jnp.bfloat16
F32 = jnp.float32


def _rms(x, g):
    ms = jnp.mean(x * x, axis=-1, keepdims=True)
    return x * lax.rsqrt(ms + EPS) * g


def _dot(a, b):
    return jnp.dot(a, b, preferred_element_type=F32)


def _resident(shape, index_map):
    return pl.BlockSpec(shape, index_map, pipeline_mode=pl.Buffered(1))


def _rope_kernel(pos_ref, inv_ref, cos_ref, sin_ref):
    p = pos_ref[0].astype(F32)
    lane = lax.broadcasted_iota(jnp.int32, (p.shape[0], LANES), 1)
    half = HEAD_DIM // 2
    pos = p[:, ROPE_PACK - 1:ROPE_PACK]
    for i in range(ROPE_PACK - 2, -1, -1):
        pos = jnp.where(lane < (i + 1) * half, p[:, i:i + 1], pos)
    ang = pos * inv_ref[...]
    cos_ref[0] = jnp.cos(ang)
    sin_ref[0] = jnp.sin(ang)


def _rope_tables(positions):
    b, s = positions.shape
    half = HEAD_DIM // 2
    inv_freq = ROPE_THETA ** (-jnp.arange(0, HEAD_DIM, 2, dtype=F32) / HEAD_DIM)
    inv = jnp.tile(inv_freq, ROPE_PACK).reshape(1, LANES)
    rows = s // ROPE_PACK
    tile = pl.BlockSpec((1, rows, LANES), lambda i: (i, 0, 0))
    cos, sin = pl.pallas_call(
        _rope_kernel,
        grid=(b,),
        in_specs=[pl.BlockSpec((1, rows, ROPE_PACK), lambda i: (i, 0, 0)),
                  pl.BlockSpec((1, LANES), lambda i: (0, 0))],
        out_specs=[tile, tile],
        out_shape=[jax.ShapeDtypeStruct((b, rows, LANES), F32)] * 2,
        name="rope_tables",
    )(positions.reshape(b, rows, ROPE_PACK), inv)
    widen = lambda t: jnp.tile(t.reshape(b, s, half), (1, 1, LANES // half))
    return widen(cos), widen(sin)


def _mix_kernel(sinks_ref, x_ref, cos_ref, sin_ref, gmix_ref, win_ref, convw_ref,
                gconv_ref, gattn_ref, wout_ref, o_ref,
                ke0, ko0, ke1, ko1, ve0, vo0, ve1, vo1, cbuf, qbuf, abuf, *, layer):
    ts = SEQ_TILE
    kbufs = ((ke0, ko0), (ke1, ko1))
    vbufs = ((ve0, vo0), (ve1, vo1))
    kv_all = (ke0, ko0, ke1, ko1, ve0, vo0, ve1, vo1)
    seq_start = pl.program_id(1) == 0

    @pl.when(seq_start)
    def _():
        for buf in kv_all:
            buf[0:WINDOW, :] = jnp.zeros((WINDOW, LANES), BF16)
        cbuf[0:CARRY_ROWS, :] = jnp.zeros((CARRY_ROWS, CONV_CH), F32)

    half_dim = HEAD_DIM // 2
    scale = 1.0 / math.sqrt(HEAD_DIM)
    w = convw_ref[...]

    qi = lax.broadcasted_iota(jnp.int32, (WINDOW, 2 * WINDOW), 0)
    ki = lax.broadcasted_iota(jnp.int32, (WINDOW, 2 * WINDOW), 1)
    diff = WINDOW + qi - ki
    band = (diff >= 0) & (diff < WINDOW)
    band_first = band & (ki >= WINDOW * seq_start.astype(jnp.int32))

    def chain(r0):
        n = SUB_TILE
        x = x_ref[0, r0:r0 + n, :]
        h = _rms(x, gmix_ref[...]).astype(BF16)

        uc = _dot(h, win_ref[:, 0:Q_OFF])
        cx = uc[:, CONV_CH:2 * CONV_CH] * uc[:, 2 * CONV_CH:3 * CONV_CH]
        c0 = CARRY_ROWS + r0
        cbuf[c0:c0 + n, :] = cx
        y = (w[2:3, :] * cx
             + w[1:2, :] * cbuf[c0 - 1:c0 - 1 + n, :]
             + w[0:1, :] * cbuf[c0 - 2:c0 - 2 + n, :])
        conv_n = _rms(uc[:, 0:CONV_CH] * y, gconv_ref[...]).astype(BF16)

        cosf = cos_ref[0, r0:r0 + n, :]
        lane = lax.broadcasted_iota(jnp.int32, (n, LANES), 1)
        first_half = (lane & half_dim) == 0
        low_head = lane < HEAD_DIM
        sin = sin_ref[0, r0:r0 + n, :]
        sins = jnp.where(first_half, -sin, sin)

        def rope(t):
            rot = jnp.where(first_half,
                            pltpu.roll(t, LANES - half_dim, 1),
                            pltpu.roll(t, half_dim, 1))
            return t * cosf + rot * sins

        uq = _dot(h, win_ref[:, Q_OFF:K_OFF])
        for c in range(ATTN_W // LANES):
            qc = (rope(uq[:, c * LANES:(c + 1) * LANES]) * scale).astype(BF16)
            for jb in range(n // WINDOW):
                qbuf[r0 // WINDOW + jb, c // 2, (c % 2) * WINDOW:(c % 2 + 1) * WINDOW, :] = (
                    qc[jb * WINDOW:(jb + 1) * WINDOW])

        ukv = _dot(h, win_ref[:, K_OFF:IN_COLS])
        k = rope(ukv[:, 0:KV_W])
        v = ukv[:, KV_W:2 * KV_W]
        k0 = WINDOW + r0
        for src, bufs in ((k, kbufs), (v, vbufs)):
            e0 = jnp.where(low_head, src, 0.0)
            o1 = jnp.where(low_head, 0.0, src)
            bufs[0][0][k0:k0 + n, :] = e0.astype(BF16)
            bufs[0][1][k0:k0 + n, :] = pltpu.roll(e0, HEAD_DIM, 1).astype(BF16)
            bufs[1][0][k0:k0 + n, :] = pltpu.roll(o1, HEAD_DIM, 1).astype(BF16)
            bufs[1][1][k0:k0 + n, :] = o1.astype(BF16)

        for jb in range(n // WINDOW):
            j = r0 // WINDOW + jb
            mask = band_first if j == 0 else band
            w0 = j * WINDOW
            for g in range(N_KV_HEADS):
                qg = qbuf[j, g]
                o = None
                for par in range(2):
                    s = lax.dot_general(qg, kbufs[g][par][w0:w0 + 2 * WINDOW, :],
                                        (((1,), (1,)), ((), ())),
                                        preferred_element_type=F32)
                    ps, inv_l = [], []
                    for sub in range(2):
                        sink = sinks_ref[layer, 4 * g + 2 * sub + par]
                        sh = jnp.where(mask, s[sub * WINDOW:(sub + 1) * WINDOW], -jnp.inf)
                        m = jnp.maximum(jnp.max(sh, axis=-1, keepdims=True), sink)
                        p = jnp.exp(sh - m)
                        l = jnp.sum(p, axis=-1, keepdims=True) + jnp.exp(sink - m)
                        ps.append(p.astype(BF16))
                        inv_l.append(1.0 / l)
                    o_par = (_dot(jnp.concatenate(ps, axis=0), vbufs[g][par][w0:w0 + 2 * WINDOW, :])
                             * jnp.concatenate(inv_l, axis=0))
                    o = o_par if o is None else o + o_par
                for sub in range(2):
                    c = 2 * g + sub
                    abuf[j * WINDOW:(j + 1) * WINDOW, c * LANES:(c + 1) * LANES] = (
                        o[sub * WINDOW:(sub + 1) * WINDOW])

        attn_n = _rms(abuf[r0:r0 + n, :], gattn_ref[...]).astype(BF16)
        o_ref[0, r0:r0 + n, :] = (x + _dot(conv_n, wout_ref[0:CONV_CH, :])
                                  + _dot(attn_n, wout_ref[CONV_CH:CONV_CH + ATTN_W, :]))

    for r0 in range(0, ts, SUB_TILE):
        chain(r0)

    cbuf[0:CARRY_ROWS, :] = cbuf[ts:ts + CARRY_ROWS, :]
    for buf in kv_all:
        buf[0:WINDOW, :] = buf[ts:ts + WINDOW, :]


def _mix_layer(layer, x, cos, sin, sinks, gmix, w_in, conv_w, gconv, gattn, w_out):
    b, s, d = x.shape
    ts = SEQ_TILE
    tok = lambda w: pl.BlockSpec((1, ts, w), lambda i, j: (i, j, 0))
    per_layer = lambda *shape: _resident((None,) + shape, lambda i, j: (layer,) + (0,) * len(shape))
    kv_scratch = [pltpu.VMEM((ts + WINDOW, LANES), BF16)] * 8
    return pl.pallas_call(
        functools.partial(_mix_kernel, layer=layer),
        grid=(b, s // ts),
        in_specs=[pl.BlockSpec(memory_space=pltpu.SMEM),
                  tok(d), tok(LANES), tok(LANES),
                  per_layer(1, d), per_layer(d, IN_COLS), per_layer(CONV_K, CONV_CH),
                  per_layer(1, CONV_CH), per_layer(1, ATTN_W), per_layer(CONV_CH + ATTN_W, d)],
        out_specs=tok(d),
        out_shape=jax.ShapeDtypeStruct((b, s, d), F32),
        scratch_shapes=kv_scratch + [
            pltpu.VMEM((ts + CARRY_ROWS, CONV_CH), F32),
            pltpu.VMEM((ts // WINDOW, N_KV_HEADS, 2 * WINDOW, LANES), BF16),
            pltpu.VMEM((ts, ATTN_W), F32)],
        compiler_params=pltpu.CompilerParams(
            dimension_semantics=("arbitrary", "arbitrary"),
            vmem_limit_bytes=VMEM_LIMIT_BYTES),
        name="token_mix",
    )(sinks, x, cos, sin, gmix, w_in, conv_w, gconv, gattn, w_out)


def _memkv_kernel(mem_ref, g_ref, wkv_ref, kt_ref, v_ref):
    memn = _rms(mem_ref[0], g_ref[...]).astype(BF16)
    kv = _dot(memn, wkv_ref[...])
    kt_ref[...] = kv[:, 0:D_MODEL].T.astype(BF16)
    v_ref[...] = kv[:, D_MODEL:2 * D_MODEL].astype(BF16)


def _mem_kv(mem, norm_mem_g, wx_kv):
    b, m, d = mem.shape
    depth = wx_kv.shape[0]
    return pl.pallas_call(
        _memkv_kernel,
        grid=(depth, b),
        in_specs=[pl.BlockSpec((1, m, d), lambda l, i: (i, 0, 0)),
                  pl.BlockSpec((None, 1, d), lambda l, i: (l, 0, 0)),
                  pl.BlockSpec((None, d, 2 * d), lambda l, i: (l, 0, 0))],
        out_specs=[pl.BlockSpec((None, None, d, m), lambda l, i: (l, i, 0, 0)),
                   pl.BlockSpec((None, None, m, d), lambda l, i: (l, i, 0, 0))],
        out_shape=[jax.ShapeDtypeStruct((depth, b, d, m), BF16),
                   jax.ShapeDtypeStruct((depth, b, m, d), BF16)],
        compiler_params=pltpu.CompilerParams(vmem_limit_bytes=VMEM_LIMIT_BYTES),
        name="mem_kv",
    )(mem, norm_mem_g, wx_kv)


def _xattn_kernel(x_ref, g_ref, wq_ref, kt_ref, v_ref, wo_ref, o_ref):
    x = x_ref[0]
    h = _rms(x, g_ref[...]).astype(BF16)
    scale = 1.0 / math.sqrt(X_HEAD_DIM)
    q = (_dot(h, wq_ref[...]) * scale).astype(BF16)
    outs = []
    for hd in range(N_X_HEADS):
        sl = slice(hd * X_HEAD_DIM, (hd + 1) * X_HEAD_DIM)
        s = _dot(q[:, sl], kt_ref[sl, :])
        m = jnp.max(s, axis=-1, keepdims=True)
        p = jnp.exp(s - m)
        l = jnp.sum(p, axis=-1, keepdims=True)
        o = _dot(p.astype(BF16), v_ref[:, sl]) * (1.0 / l)
        outs.append(o.astype(BF16))
    o_ref[0] = x + _dot(jnp.concatenate(outs, axis=1), wo_ref[...])


def _xattn_layer(layer, x, g, wq, kt, v, wo):
    b, s, d = x.shape
    ts = SEQ_TILE
    tok = pl.BlockSpec((1, ts, d), lambda i, j: (i, j, 0))
    per_layer = lambda *shape: _resident((None,) + shape, lambda i, j: (layer,) + (0,) * len(shape))
    return pl.pallas_call(
        _xattn_kernel,
        grid=(b, s // ts),
        in_specs=[tok, per_layer(1, d), per_layer(d, d),
                  pl.BlockSpec((None, None, d, N_MEM), lambda i, j: (layer, i, 0, 0)),
                  pl.BlockSpec((None, None, N_MEM, d), lambda i, j: (layer, i, 0, 0)),
                  per_layer(d, d)],
        out_specs=tok,
        out_shape=jax.ShapeDtypeStruct((b, s, d), F32),
        compiler_params=pltpu.CompilerParams(vmem_limit_bytes=VMEM_LIMIT_BYTES),
        name="mem_xattn",
    )(x, g, wq, kt, v, wo)


def _mlp_kernel(x_ref, g_ref, wup_ref, wdn_ref, *rest, final):
    o_ref = rest[-1]
    x = x_ref[...]
    h = _rms(x, g_ref[...]).astype(BF16)
    acc = x
    for c in range(D_FF // FF_CHUNK):
        sl = slice(c * FF_CHUNK, (c + 1) * FF_CHUNK)
        up = _dot(h, wup_ref[:, sl])
        act = jnp.square(jnp.maximum(up, 0.0)).astype(BF16)
        acc = acc + _dot(act, wdn_ref[sl, :])
    if final:
        acc = _rms(acc, rest[0][...])
    o_ref[...] = acc


def _mlp_layer(layer, x2d, g, w_up, w_down, final_g):
    n, d = x2d.shape
    tm = MLP_TILE
    final = final_g is not None
    tok = pl.BlockSpec((tm, d), lambda i: (i, 0))
    per_layer = lambda *shape: _resident((None,) + shape, lambda i: (layer,) + (0,) * len(shape))
    in_specs = [tok, per_layer(1, d), per_layer(d, D_FF), per_layer(D_FF, d)]
    args = [x2d, g, w_up, w_down]
    if final:
        in_specs.append(_resident((1, d), lambda i: (0, 0)))
        args.append(final_g.reshape(1, d))
    return pl.pallas_call(
        functools.partial(_mlp_kernel, final=final),
        grid=(n // tm,),
        in_specs=in_specs,
        out_specs=tok,
        out_shape=jax.ShapeDtypeStruct((n, d), F32),
        compiler_params=pltpu.CompilerParams(vmem_limit_bytes=VMEM_LIMIT_BYTES),
        name="relu2_mlp",
    )(*args)


def kernel(x, mem, positions, norm_mix_g, w_in, conv_w, sinks, gnorm_conv_g, gnorm_attn_g, w_out, norm_x_g, norm_mem_g, wx_q, wx_kv, wx_o, norm_mlp_g, w_up, w_down, final_g):
    b, s, d = x.shape
    depth = w_in.shape[0]
    row = lambda g: g.reshape(depth, 1, g.shape[-1])
    cos, sin = _rope_tables(positions)
    kt, v = _mem_kv(mem, row(norm_mem_g), wx_kv.astype(BF16))
    w_in, w_out, wx_q, wx_o, w_up, w_down = (
        t.astype(BF16) for t in (w_in, w_out, wx_q, wx_o, w_up, w_down))
    for l in range(depth):
        x = _mix_layer(l, x, cos, sin, sinks, row(norm_mix_g), w_in, conv_w,
                       row(gnorm_conv_g), row(gnorm_attn_g), w_out)
        x = _xattn_layer(l, x, row(norm_x_g), wx_q, kt, v, wx_o)
        x = _mlp_layer(l, x.reshape(b * s, d), row(norm_mlp_g), w_up, w_down,
                       final_g if l == depth - 1 else None).reshape(b, s, d)
    return x
```

```python
import functools
import math

import jax
import jax.numpy as jnp
from jax import lax
from jax.experimental import pallas as pl
from jax.experimental.pallas import tpu as pltpu

D_MODEL = 1024
N_MEM = 256
CONV_CH = 512
CONV_K = 3
HEAD_DIM = 64
N_Q_HEADS = 8
N_KV_HEADS = 2
ATTN_W = N_Q_HEADS * HEAD_DIM
KV_W = N_KV_HEADS * HEAD_DIM
WINDOW = 128
ROPE_THETA = 10000.0
N_X_HEADS = 4
X_HEAD_DIM = D_MODEL // N_X_HEADS
D_FF = 4 * D_MODEL
EPS = 1e-6
LOG2E = math.log2(math.e)
IN_COLS = 3 * CONV_CH + ATTN_W + 2 * KV_W
Q_OFF = 3 * CONV_CH
K_OFF = Q_OFF + ATTN_W

LANES = 128
SUBLANES = 8
VMEM_LIMIT_BYTES = 56 * 1024 * 1024

SEQ_TILE = 512
SUB_TILE = 256
MLP_TILE = 1024
FF_CHUNK = 1024
CARRY_ROWS = SUBLANES

BF16 = jnp.bfloat16
F32 = jnp.float32


def _rms(x, g):
    ms = jnp.mean(x * x, axis=-1, keepdims=True)
    return x * lax.rsqrt(ms + EPS) * g


def _dot(a, b):
    return jnp.dot(a, b, preferred_element_type=F32)


def _resident(shape, index_map):
    return pl.BlockSpec(shape, index_map, pipeline_mode=pl.Buffered(1))


def _rope_kernel(pos_ref, inv_ref, cos_ref, sin_ref):
    ang = inv_ref[...] * pos_ref[0].astype(F32)
    c = jnp.cos(ang)
    s = jnp.sin(ang)
    cos_ref[0] = jnp.concatenate([c, c, c, c], axis=0).T
    sin_ref[0] = jnp.concatenate([-s, s, -s, s], axis=0).T


def _rope_tables(positions):
    b, s = positions.shape
    half = HEAD_DIM // 2
    inv_freq = ROPE_THETA ** (-jnp.arange(0, HEAD_DIM, 2, dtype=F32) / HEAD_DIM)
    table = pl.BlockSpec((1, s, LANES), lambda i: (i, 0, 0))
    return pl.pallas_call(
        _rope_kernel,
        grid=(b,),
        in_specs=[pl.BlockSpec((1, 1, s), lambda i: (i, 0, 0)),
                  pl.BlockSpec((half, 1), lambda i: (0, 0))],
        out_specs=[table, table],
        out_shape=[jax.ShapeDtypeStruct((b, s, LANES), F32)] * 2,
        name="rope_tables",
    )(positions.reshape(b, 1, s), inv_freq.reshape(half, 1))


def _mix_kernel(sinks_ref, x_ref, cos_ref, sin_ref, gmix_ref, win_ref, convw_ref,
                gconv_ref, gattn_ref, wout_ref, o_ref,
                ke0, ko0, ke1, ko1, ve0, vo0, ve1, vo1, cbuf, qbuf, abuf, *, layer):
    ts = SEQ_TILE
    kbufs = ((ke0, ko0), (ke1, ko1))
    vbufs = ((ve0, vo0), (ve1, vo1))
    kv_all = (ke0, ko0, ke1, ko1, ve0, vo0, ve1, vo1)
    seq_start = pl.program_id(1) == 0

    @pl.when(seq_start)
    def _():
        for buf in kv_all:
            buf[0:WINDOW, :] = jnp.zeros((WINDOW, LANES), BF16)
        cbuf[0:CARRY_ROWS, :] = jnp.zeros((CARRY_ROWS, CONV_CH), F32)

    half_dim = HEAD_DIM // 2
    scale = LOG2E / math.sqrt(HEAD_DIM)
    w = convw_ref[...]

    qi = lax.broadcasted_iota(jnp.int32, (WINDOW, WINDOW), 0)
    ki = lax.broadcasted_iota(jnp.int32, (WINDOW, WINDOW), 1)
    own = ki <= qi
    no_prev = ki < WINDOW * seq_start.astype(jnp.int32)

    def chain(r0):
        n = SUB_TILE
        x = x_ref[0, r0:r0 + n, :]
        h = _rms(x, gmix_ref[...]).astype(BF16)

        uc = _dot(h, win_ref[:, 0:Q_OFF])
        cx = uc[:, CONV_CH:2 * CONV_CH] * uc[:, 2 * CONV_CH:3 * CONV_CH]
        c0 = CARRY_ROWS + r0
        cbuf[c0:c0 + n, :] = cx
        y = (w[2:3, :] * cx
             + w[1:2, :] * cbuf[c0 - 1:c0 - 1 + n, :]
             + w[0:1, :] * cbuf[c0 - 2:c0 - 2 + n, :])
        conv_n = _rms(uc[:, 0:CONV_CH] * y, gconv_ref[...]).astype(BF16)

        cosf = cos_ref[0, r0:r0 + n, :]
        lane = lax.broadcasted_iota(jnp.int32, (n, LANES), 1)
        first_half = (lane & half_dim) == 0
        low_head = lane < HEAD_DIM
        sins = sin_ref[0, r0:r0 + n, :]

        def rope(t):
            rot = jnp.where(first_half,
                            pltpu.roll(t, LANES - half_dim, 1),
                            pltpu.roll(t, half_dim, 1))
            return t * cosf + rot * sins

        uq = _dot(h, win_ref[:, Q_OFF:K_OFF])
        for c in range(ATTN_W // LANES):
            qc = (rope(uq[:, c * LANES:(c + 1) * LANES]) * scale).astype(BF16)
            for jb in range(n // WINDOW):
                qbuf[r0 // WINDOW + jb, c // 2, (c % 2) * WINDOW:(c % 2 + 1) * WINDOW, :] = (
                    qc[jb * WINDOW:(jb + 1) * WINDOW])

        ukv = _dot(h, win_ref[:, K_OFF:IN_COLS])
        k = rope(ukv[:, 0:KV_W])
        v = ukv[:, KV_W:2 * KV_W]
        k0 = WINDOW + r0
        for src, bufs in ((k, kbufs), (v, vbufs)):
            e0 = jnp.where(low_head, src, 0.0)
            o1 = jnp.where(low_head, 0.0, src)
            bufs[0][0][k0:k0 + n, :] = e0.astype(BF16)
            bufs[0][1][k0:k0 + n, :] = pltpu.roll(e0, HEAD_DIM, 1).astype(BF16)
            bufs[1][0][k0:k0 + n, :] = pltpu.roll(o1, HEAD_DIM, 1).astype(BF16)
            bufs[1][1][k0:k0 + n, :] = o1.astype(BF16)

        for jb in range(n // WINDOW):
            j = r0 // WINDOW + jb
            w0 = j * WINDOW
            for g in range(N_KV_HEADS):
                qg = qbuf[j, g]
                o = None
                for par in range(2):
                    s = lax.dot_general(qg, kbufs[g][par][w0:w0 + 2 * WINDOW, :],
                                        (((1,), (1,)), ((), ())),
                                        preferred_element_type=F32)
                    ps, inv_l = [], []
                    for sub in range(2):
                        sink = sinks_ref[layer, 4 * g + 2 * sub + par] * LOG2E
                        s_prev = s[sub * WINDOW:(sub + 1) * WINDOW, 0:WINDOW]
                        s_own = s[sub * WINDOW:(sub + 1) * WINDOW, WINDOW:2 * WINDOW]
                        if j == 0:
                            s_prev = jnp.where(no_prev, -jnp.inf, s_prev)
                        sh = jnp.where(own, s_own, s_prev)
                        m = jnp.maximum(jnp.max(sh, axis=-1, keepdims=True), sink)
                        p = jnp.exp2(sh - m)
                        l = jnp.sum(p, axis=-1, keepdims=True) + jnp.exp2(sink - m)
                        p_own = jnp.where(own, p, 0.0)
                        ps.append(jnp.concatenate([p - p_own, p_own], axis=1).astype(BF16))
                        inv_l.append(1.0 / l)
                    o_par = (_dot(jnp.concatenate(ps, axis=0), vbufs[g][par][w0:w0 + 2 * WINDOW, :])
                             * jnp.concatenate(inv_l, axis=0))
                    o = o_par if o is None else o + o_par
                for sub in range(2):
                    c = 2 * g + sub
                    abuf[j * WINDOW:(j + 1) * WINDOW, c * LANES:(c + 1) * LANES] = (
                        o[sub * WINDOW:(sub + 1) * WINDOW])

        attn_n = _rms(abuf[r0:r0 + n, :], gattn_ref[...]).astype(BF16)
        o_ref[0, r0:r0 + n, :] = (x + _dot(conv_n, wout_ref[0:CONV_CH, :])
                                  + _dot(attn_n, wout_ref[CONV_CH:CONV_CH + ATTN_W, :]))

    for r0 in range(0, ts, SUB_TILE):
        chain(r0)

    cbuf[0:CARRY_ROWS, :] = cbuf[ts:ts + CARRY_ROWS, :]
    for buf in kv_all:
        buf[0:WINDOW, :] = buf[ts:ts + WINDOW, :]


def _mix_layer(layer, x, cos, sin, sinks, gmix, w_in, conv_w, gconv, gattn, w_out):
    b, s, d = x.shape
    ts = SEQ_TILE
    tok = lambda w: pl.BlockSpec((1, ts, w), lambda i, j: (i, j, 0))
    per_layer = lambda *shape: _resident((None,) + shape, lambda i, j: (layer,) + (0,) * len(shape))
    kv_scratch = [pltpu.VMEM((ts + WINDOW, LANES), BF16)] * 8
    return pl.pallas_call(
        functools.partial(_mix_kernel, layer=layer),
        grid=(b, s // ts),
        in_specs=[pl.BlockSpec(memory_space=pltpu.SMEM),
                  tok(d), tok(LANES), tok(LANES),
                  per_layer(1, d), per_layer(d, IN_COLS), per_layer(CONV_K, CONV_CH),
                  per_layer(1, CONV_CH), per_layer(1, ATTN_W), per_layer(CONV_CH + ATTN_W, d)],
        out_specs=tok(d),
        out_shape=jax.ShapeDtypeStruct((b, s, d), F32),
        scratch_shapes=kv_scratch + [
            pltpu.VMEM((ts + CARRY_ROWS, CONV_CH), F32),
            pltpu.VMEM((ts // WINDOW, N_KV_HEADS, 2 * WINDOW, LANES), BF16),
            pltpu.VMEM((ts, ATTN_W), F32)],
        compiler_params=pltpu.CompilerParams(
            dimension_semantics=("arbitrary", "arbitrary"),
            vmem_limit_bytes=VMEM_LIMIT_BYTES),
        name="token_mix",
    )(sinks, x, cos, sin, gmix, w_in, conv_w, gconv, gattn, w_out)


def _memkv_kernel(mem_ref, g_ref, wkv_ref, kt_ref, v_ref):
    memn = _rms(mem_ref[0], g_ref[...]).astype(BF16)
    kv = _dot(memn, wkv_ref[...])
    kt_ref[...] = kv[:, 0:D_MODEL].T.astype(BF16)
    v_ref[...] = kv[:, D_MODEL:2 * D_MODEL].astype(BF16)


def _mem_kv(mem, norm_mem_g, wx_kv):
    b, m, d = mem.shape
    depth = wx_kv.shape[0]
    return pl.pallas_call(
        _memkv_kernel,
        grid=(depth, b),
        in_specs=[pl.BlockSpec((1, m, d), lambda l, i: (i, 0, 0)),
                  pl.BlockSpec((None, 1, d), lambda l, i: (l, 0, 0)),
                  pl.BlockSpec((None, d, 2 * d), lambda l, i: (l, 0, 0))],
        out_specs=[pl.BlockSpec((None, None, d, m), lambda l, i: (l, i, 0, 0)),
                   pl.BlockSpec((None, None, m, d), lambda l, i: (l, i, 0, 0))],
        out_shape=[jax.ShapeDtypeStruct((depth, b, d, m), BF16),
                   jax.ShapeDtypeStruct((depth, b, m, d), BF16)],
        compiler_params=pltpu.CompilerParams(vmem_limit_bytes=VMEM_LIMIT_BYTES),
        name="mem_kv",
    )(mem, norm_mem_g, wx_kv)


def _xattn_kernel(x_ref, g_ref, wq_ref, kt_ref, v_ref, wo_ref, o_ref):
    x = x_ref[0]
    h = _rms(x, g_ref[...]).astype(BF16)
    scale = 1.0 / math.sqrt(X_HEAD_DIM)
    q = (_dot(h, wq_ref[...]) * scale).astype(BF16)
    outs = []
    for hd in range(N_X_HEADS):
        sl = slice(hd * X_HEAD_DIM, (hd + 1) * X_HEAD_DIM)
        s = _dot(q[:, sl], kt_ref[sl, :])
        m = jnp.max(s, axis=-1, keepdims=True)
        p = jnp.exp(s - m)
        l = jnp.sum(p, axis=-1, keepdims=True)
        o = _dot(p.astype(BF16), v_ref[:, sl]) * (1.0 / l)
        outs.append(o.astype(BF16))
    o_ref[0] = x + _dot(jnp.concatenate(outs, axis=1), wo_ref[...])


def _xattn_layer(layer, x, g, wq, kt, v, wo):
    b, s, d = x.shape
    ts = SEQ_TILE
    tok = pl.BlockSpec((1, ts, d), lambda i, j: (i, j, 0))
    per_layer = lambda *shape: _resident((None,) + shape, lambda i, j: (layer,) + (0,) * len(shape))
    return pl.pallas_call(
        _xattn_kernel,
        grid=(b, s // ts),
        in_specs=[tok, per_layer(1, d), per_layer(d, d),
                  pl.BlockSpec((None, None, d, N_MEM), lambda i, j: (layer, i, 0, 0)),
                  pl.BlockSpec((None, None, N_MEM, d), lambda i, j: (layer, i, 0, 0)),
                  per_layer(d, d)],
        out_specs=tok,
        out_shape=jax.ShapeDtypeStruct((b, s, d), F32),
        compiler_params=pltpu.CompilerParams(vmem_limit_bytes=VMEM_LIMIT_BYTES),
        name="mem_xattn",
    )(x, g, wq, kt, v, wo)


def _mlp_kernel(x_ref, g_ref, wup_ref, wdn_ref, *rest, final):
    o_ref = rest[-1]
    x = x_ref[...]
    h = _rms(x, g_ref[...]).astype(BF16)
    acc = x
    for c in range(D_FF // FF_CHUNK):
        sl = slice(c * FF_CHUNK, (c + 1) * FF_CHUNK)
        up = _dot(h, wup_ref[:, sl])
        act = jnp.square(jnp.maximum(up, 0.0)).astype(BF16)
        acc = acc + _dot(act, wdn_ref[sl, :])
    if final:
        acc = _rms(acc, rest[0][...])
    o_ref[...] = acc


def _mlp_layer(layer, x2d, g, w_up, w_down, final_g):
    n, d = x2d.shape
    tm = MLP_TILE
    final = final_g is not None
    tok = pl.BlockSpec((tm, d), lambda i: (i, 0))
    per_layer = lambda *shape: _resident((None,) + shape, lambda i: (layer,) + (0,) * len(shape))
    in_specs = [tok, per_layer(1, d), per_layer(d, D_FF), per_layer(D_FF, d)]
    args = [x2d, g, w_up, w_down]
    if final:
        in_specs.append(_resident((1, d), lambda i: (0, 0)))
        args.append(final_g.reshape(1, d))
    return pl.pallas_call(
        functools.partial(_mlp_kernel, final=final),
        grid=(n // tm,),
        in_specs=in_specs,
        out_specs=tok,
        out_shape=jax.ShapeDtypeStruct((n, d), F32),
        compiler_params=pltpu.CompilerParams(vmem_limit_bytes=VMEM_LIMIT_BYTES),
        name="relu2_mlp",
    )(*args)


def kernel(x, mem, positions, norm_mix_g, w_in, conv_w, sinks, gnorm_conv_g, gnorm_attn_g, w_out, norm_x_g, norm_mem_g, wx_q, wx_kv, wx_o, norm_mlp_g, w_up, w_down, final_g):
    b, s, d = x.shape
    depth = w_in.shape[0]
    row = lambda g: g.reshape(depth, 1, g.shape[-1])
    cos, sin = _rope_tables(positions)
    kt, v = _mem_kv(mem, row(norm_mem_g), wx_kv.astype(BF16))
    w_in, w_out, wx_q, wx_o, w_up, w_down = (
        t.astype(BF16) for t in (w_in, w_out, wx_q, wx_o, w_up, w_down))
    for l in range(depth):
        x = _mix_layer(l, x, cos, sin, sinks, row(norm_mix_g), w_in, conv_w,
                       row(gnorm_conv_g), row(gnorm_attn_g), w_out)
        x = _xattn_layer(l, x, row(norm_x_g), wx_q, kt, v, wx_o)
        x = _mlp_layer(l, x.reshape(b * s, d), row(norm_mlp_g), w_up, w_down,
                       final_g if l == depth - 1 else None).reshape(b, s, d)
    return x
```

```python
import functools
import math

import jax
import jax.numpy as jnp
from jax import lax
from jax.experimental import pallas as pl
from jax.experimental.pallas import tpu as pltpu

D_MODEL = 1024
N_MEM = 256
CONV_CH = 512
CONV_K = 3
HEAD_DIM = 64
N_Q_HEADS = 8
N_KV_HEADS = 2
ATTN_W = N_Q_HEADS * HEAD_DIM
KV_W = N_KV_HEADS * HEAD_DIM
WINDOW = 128
ROPE_THETA = 10000.0
N_X_HEADS = 4
X_HEAD_DIM = D_MODEL // N_X_HEADS
D_FF = 4 * D_MODEL
EPS = 1e-6
LOG2E = math.log2(math.e)
IN_COLS = 3 * CONV_CH + ATTN_W + 2 * KV_W
Q_OFF = 3 * CONV_CH
K_OFF = Q_OFF + ATTN_W

LANES = 128
SUBLANES = 8
VMEM_LIMIT_BYTES = 56 * 1024 * 1024

SEQ_TILE = 512
SUB_TILE = 256
MLP_TILE = 512
FF_CHUNK = 1024
CARRY_ROWS = SUBLANES

BF16 = jnp.bfloat16
F32 = jnp.float32


def _rms(x, g):
    ms = jnp.mean(x * x, axis=-1, keepdims=True)
    return x * lax.rsqrt(ms + EPS) * g


def _dot(a, b):
    return jnp.dot(a, b.astype(BF16), preferred_element_type=F32)


def _resident(shape, index_map):
    return pl.BlockSpec(shape, index_map, pipeline_mode=pl.Buffered(1))


def _rope_kernel(pos_ref, inv_ref, cos_ref, sin_ref):
    ang = inv_ref[...] * pos_ref[0].astype(F32)
    c = jnp.cos(ang)
    s = jnp.sin(ang)
    cos_ref[0] = jnp.concatenate([c, c, c, c], axis=0).T
    sin_ref[0] = jnp.concatenate([-s, s, -s, s], axis=0).T


def _rope_tables(positions):
    b, s = positions.shape
    half = HEAD_DIM // 2
    inv_freq = ROPE_THETA ** (-jnp.arange(0, HEAD_DIM, 2, dtype=F32) / HEAD_DIM)
    table = pl.BlockSpec((1, s, LANES), lambda i: (i, 0, 0))
    return pl.pallas_call(
        _rope_kernel,
        grid=(b,),
        in_specs=[pl.BlockSpec((1, 1, s), lambda i: (i, 0, 0)),
                  pl.BlockSpec((half, 1), lambda i: (0, 0))],
        out_specs=[table, table],
        out_shape=[jax.ShapeDtypeStruct((b, s, LANES), F32)] * 2,
        name="rope_tables",
    )(positions.reshape(b, 1, s), inv_freq.reshape(half, 1))


def _mix_kernel(sinks_ref, x_ref, cos_ref, sin_ref, gmix_ref, win_ref, convw_ref,
                gconv_ref, gattn_ref, wout_ref, o_ref,
                ke0, ko0, ke1, ko1, ve0, vo0, ve1, vo1, cbuf, qbuf, abuf, *, layer):
    ts = SEQ_TILE
    kbufs = ((ke0, ko0), (ke1, ko1))
    vbufs = ((ve0, vo0), (ve1, vo1))
    kv_all = (ke0, ko0, ke1, ko1, ve0, vo0, ve1, vo1)
    seq_start = pl.program_id(1) == 0

    @pl.when(seq_start)
    def _():
        for buf in kv_all:
            buf[0:WINDOW, :] = jnp.zeros((WINDOW, LANES), BF16)
        cbuf[0:CARRY_ROWS, :] = jnp.zeros((CARRY_ROWS, CONV_CH), F32)

    half_dim = HEAD_DIM // 2
    scale = LOG2E / math.sqrt(HEAD_DIM)
    w = convw_ref[...]

    qi = lax.broadcasted_iota(jnp.int32, (WINDOW, WINDOW), 0)
    ki = lax.broadcasted_iota(jnp.int32, (WINDOW, WINDOW), 1)
    own = ki <= qi
    no_prev = ki < WINDOW * seq_start.astype(jnp.int32)

    def chain(r0):
        n = SUB_TILE
        x = x_ref[0, r0:r0 + n, :]
        h = _rms(x, gmix_ref[...]).astype(BF16)

        uc = _dot(h, win_ref[:, 0:Q_OFF])
        cx = uc[:, CONV_CH:2 * CONV_CH] * uc[:, 2 * CONV_CH:3 * CONV_CH]
        c0 = CARRY_ROWS + r0
        cbuf[c0:c0 + n, :] = cx
        y = (w[2:3, :] * cx
             + w[1:2, :] * cbuf[c0 - 1:c0 - 1 + n, :]
             + w[0:1, :] * cbuf[c0 - 2:c0 - 2 + n, :])
        conv_n = _rms(uc[:, 0:CONV_CH] * y, gconv_ref[...]).astype(BF16)

        cosf = cos_ref[0, r0:r0 + n, :]
        lane = lax.broadcasted_iota(jnp.int32, (n, LANES), 1)
        first_half = (lane & half_dim) == 0
        low_head = lane < HEAD_DIM
        sins = sin_ref[0, r0:r0 + n, :]

        def rope(t):
            rot = jnp.where(first_half,
                            pltpu.roll(t, LANES - half_dim, 1),
                            pltpu.roll(t, half_dim, 1))
            return t * cosf + rot * sins

        uq = _dot(h, win_ref[:, Q_OFF:K_OFF])
        for c in range(ATTN_W // LANES):
            qc = (rope(uq[:, c * LANES:(c + 1) * LANES]) * scale).astype(BF16)
            for jb in range(n // WINDOW):
                qbuf[r0 // WINDOW + jb, c // 2, (c % 2) * WINDOW:(c % 2 + 1) * WINDOW, :] = (
                    qc[jb * WINDOW:(jb + 1) * WINDOW])

        ukv = _dot(h, win_ref[:, K_OFF:IN_COLS])
        k = rope(ukv[:, 0:KV_W])
        v = ukv[:, KV_W:2 * KV_W]
        k0 = WINDOW + r0
        for src, bufs in ((k, kbufs), (v, vbufs)):
            e0 = jnp.where(low_head, src, 0.0)
            o1 = jnp.where(low_head, 0.0, src)
            bufs[0][0][k0:k0 + n, :] = e0.astype(BF16)
            bufs[0][1][k0:k0 + n, :] = pltpu.roll(e0, HEAD_DIM, 1).astype(BF16)
            bufs[1][0][k0:k0 + n, :] = pltpu.roll(o1, HEAD_DIM, 1).astype(BF16)
            bufs[1][1][k0:k0 + n, :] = o1.astype(BF16)

        for jb in range(n // WINDOW):
            j = r0 // WINDOW + jb
            w0 = j * WINDOW
            for g in range(N_KV_HEADS):
                qg = qbuf[j, g]
                o = None
                for par in range(2):
                    s = lax.dot_general(qg, kbufs[g][par][w0:w0 + 2 * WINDOW, :],
                                        (((1,), (1,)), ((), ())),
                                        preferred_element_type=F32)
                    ps, inv_l = [], []
                    for sub in range(2):
                        sink = sinks_ref[layer, 4 * g + 2 * sub + par] * LOG2E
                        s_prev = s[sub * WINDOW:(sub + 1) * WINDOW, 0:WINDOW]
                        s_own = s[sub * WINDOW:(sub + 1) * WINDOW, WINDOW:2 * WINDOW]
                        if j == 0:
                            s_prev = jnp.where(no_prev, -jnp.inf, s_prev)
                        sh = jnp.where(own, s_own, s_prev)
                        m = jnp.maximum(jnp.max(sh, axis=-1, keepdims=True), sink)
                        p = jnp.exp2(sh - m)
                        l = jnp.sum(p, axis=-1, keepdims=True) + jnp.exp2(sink - m)
                        p_own = jnp.where(own, p, 0.0)
                        ps.append(jnp.concatenate([p - p_own, p_own], axis=1).astype(BF16))
                        inv_l.append(1.0 / l)
                    o_par = (_dot(jnp.concatenate(ps, axis=0), vbufs[g][par][w0:w0 + 2 * WINDOW, :])
                             * jnp.concatenate(inv_l, axis=0))
                    o = o_par if o is None else o + o_par
                for sub in range(2):
                    c = 2 * g + sub
                    abuf[j * WINDOW:(j + 1) * WINDOW, c * LANES:(c + 1) * LANES] = (
                        o[sub * WINDOW:(sub + 1) * WINDOW])

        attn_n = _rms(abuf[r0:r0 + n, :], gattn_ref[...]).astype(BF16)
        o_ref[0, r0:r0 + n, :] = (x + _dot(conv_n, wout_ref[0:CONV_CH, :])
                                  + _dot(attn_n, wout_ref[CONV_CH:CONV_CH + ATTN_W, :]))

    for r0 in range(0, ts, SUB_TILE):
        chain(r0)

    cbuf[0:CARRY_ROWS, :] = cbuf[ts:ts + CARRY_ROWS, :]
    for buf in kv_all:
        buf[0:WINDOW, :] = buf[ts:ts + WINDOW, :]


def _mix_layer(layer, x, cos, sin, sinks, gmix, w_in, conv_w, gconv, gattn, w_out):
    b, s, d = x.shape
    ts = SEQ_TILE
    tok = lambda w: pl.BlockSpec((1, ts, w), lambda i, j: (i, j, 0))
    per_layer = lambda *shape: _resident((None,) + shape, lambda i, j: (layer,) + (0,) * len(shape))
    kv_scratch = [pltpu.VMEM((ts + WINDOW, LANES), BF16)] * 8
    return pl.pallas_call(
        functools.partial(_mix_kernel, layer=layer),
        grid=(b, s // ts),
        in_specs=[pl.BlockSpec(memory_space=pltpu.SMEM),
                  tok(d), tok(LANES), tok(LANES),
                  per_layer(1, d), per_layer(d, IN_COLS), per_layer(CONV_K, CONV_CH),
                  per_layer(1, CONV_CH), per_layer(1, ATTN_W), per_layer(CONV_CH + ATTN_W, d)],
        out_specs=tok(d),
        out_shape=jax.ShapeDtypeStruct((b, s, d), F32),
        scratch_shapes=kv_scratch + [
            pltpu.VMEM((ts + CARRY_ROWS, CONV_CH), F32),
            pltpu.VMEM((ts // WINDOW, N_KV_HEADS, 2 * WINDOW, LANES), BF16),
            pltpu.VMEM((ts, ATTN_W), F32)],
        compiler_params=pltpu.CompilerParams(
            dimension_semantics=("arbitrary", "arbitrary"),
            vmem_limit_bytes=VMEM_LIMIT_BYTES),
        name="token_mix",
    )(sinks, x, cos, sin, gmix, w_in, conv_w, gconv, gattn, w_out)


def _memkv_kernel(mem_ref, g_ref, wkv_ref, kt_ref, v_ref):
    memn = _rms(mem_ref[0], g_ref[...]).astype(BF16)
    kv = _dot(memn, wkv_ref[...])
    kt_ref[...] = kv[:, 0:D_MODEL].T.astype(BF16)
    v_ref[...] = kv[:, D_MODEL:2 * D_MODEL].astype(BF16)


def _mem_kv(mem, norm_mem_g, wx_kv):
    b, m, d = mem.shape
    depth = wx_kv.shape[0]
    return pl.pallas_call(
        _memkv_kernel,
        grid=(depth, b),
        in_specs=[pl.BlockSpec((1, m, d), lambda l, i: (i, 0, 0)),
                  pl.BlockSpec((None, 1, d), lambda l, i: (l, 0, 0)),
                  pl.BlockSpec((None, d, 2 * d), lambda l, i: (l, 0, 0))],
        out_specs=[pl.BlockSpec((None, None, d, m), lambda l, i: (l, i, 0, 0)),
                   pl.BlockSpec((None, None, m, d), lambda l, i: (l, i, 0, 0))],
        out_shape=[jax.ShapeDtypeStruct((depth, b, d, m), BF16),
                   jax.ShapeDtypeStruct((depth, b, m, d), BF16)],
        compiler_params=pltpu.CompilerParams(vmem_limit_bytes=VMEM_LIMIT_BYTES),
        name="mem_kv",
    )(mem, norm_mem_g, wx_kv)


def _xattn_kernel(x_ref, g_ref, wq_ref, kt_ref, v_ref, wo_ref, o_ref):
    x = x_ref[0]
    h = _rms(x, g_ref[...]).astype(BF16)
    scale = 1.0 / math.sqrt(X_HEAD_DIM)
    q = (_dot(h, wq_ref[...]) * scale).astype(BF16)
    outs = []
    for hd in range(N_X_HEADS):
        sl = slice(hd * X_HEAD_DIM, (hd + 1) * X_HEAD_DIM)
        s = _dot(q[:, sl], kt_ref[sl, :])
        m = jnp.max(s, axis=-1, keepdims=True)
        p = jnp.exp(s - m)
        l = jnp.sum(p, axis=-1, keepdims=True)
        o = _dot(p.astype(BF16), v_ref[:, sl]) * (1.0 / l)
        outs.append(o.astype(BF16))
    o_ref[0] = x + _dot(jnp.concatenate(outs, axis=1), wo_ref[...])


def _xattn_layer(layer, x, g, wq, kt, v, wo):
    b, s, d = x.shape
    ts = SEQ_TILE
    tok = pl.BlockSpec((1, ts, d), lambda i, j: (i, j, 0))
    per_layer = lambda *shape: _resident((None,) + shape, lambda i, j: (layer,) + (0,) * len(shape))
    return pl.pallas_call(
        _xattn_kernel,
        grid=(b, s // ts),
        in_specs=[tok, per_layer(1, d), per_layer(d, d),
                  pl.BlockSpec((None, None, d, N_MEM), lambda i, j: (layer, i, 0, 0)),
                  pl.BlockSpec((None, None, N_MEM, d), lambda i, j: (layer, i, 0, 0)),
                  per_layer(d, d)],
        out_specs=tok,
        out_shape=jax.ShapeDtypeStruct((b, s, d), F32),
        compiler_params=pltpu.CompilerParams(vmem_limit_bytes=VMEM_LIMIT_BYTES),
        name="mem_xattn",
    )(x, g, wq, kt, v, wo)


def _mlp_kernel(x_ref, g_ref, wup_ref, wdn_ref, *rest, final):
    o_ref = rest[-1]
    x = x_ref[...]
    h = _rms(x, g_ref[...]).astype(BF16)
    acc = x
    for c in range(D_FF // FF_CHUNK):
        sl = slice(c * FF_CHUNK, (c + 1) * FF_CHUNK)
        up = _dot(h, wup_ref[:, sl])
        act = jnp.square(jnp.maximum(up, 0.0)).astype(BF16)
        acc = acc + _dot(act, wdn_ref[sl, :])
    if final:
        acc = _rms(acc, rest[0][...])
    o_ref[...] = acc


def _mlp_layer(layer, x2d, g, w_up, w_down, final_g):
    n, d = x2d.shape
    tm = MLP_TILE
    final = final_g is not None
    tok = pl.BlockSpec((tm, d), lambda i: (i, 0))
    per_layer = lambda *shape: _resident((None,) + shape, lambda i: (layer,) + (0,) * len(shape))
    in_specs = [tok, per_layer(1, d), per_layer(d, D_FF), per_layer(D_FF, d)]
    args = [x2d, g, w_up, w_down]
    if final:
        in_specs.append(_resident((1, d), lambda i: (0, 0)))
        args.append(final_g.reshape(1, d))
    return pl.pallas_call(
        functools.partial(_mlp_kernel, final=final),
        grid=(n // tm,),
        in_specs=in_specs,
        out_specs=tok,
        out_shape=jax.ShapeDtypeStruct((n, d), F32),
        compiler_params=pltpu.CompilerParams(vmem_limit_bytes=VMEM_LIMIT_BYTES),
        name="relu2_mlp",
    )(*args)


def kernel(x, mem, positions, norm_mix_g, w_in, conv_w, sinks, gnorm_conv_g, gnorm_attn_g, w_out, norm_x_g, norm_mem_g, wx_q, wx_kv, wx_o, norm_mlp_g, w_up, w_down, final_g):
    b, s, d = x.shape
    depth = w_in.shape[0]
    row = lambda g: g.reshape(depth, 1, g.shape[-1])
    cos, sin = _rope_tables(positions)
    kt, v = _mem_kv(mem, row(norm_mem_g), wx_kv)
    for l in range(depth):
        x = _mix_layer(l, x, cos, sin, sinks, row(norm_mix_g), w_in, conv_w,
                       row(gnorm_conv_g), row(gnorm_attn_g), w_out)
        x = _xattn_layer(l, x, row(norm_x_g), wx_q, kt, v, wx_o)
        x = _mlp_layer(l, x.reshape(b * s, d), row(norm_mlp_g), w_up, w_down,
                       final_g if l == depth - 1 else None).reshape(b, s, d)
    return x
```

```python
import functools
import math

import jax
import jax.numpy as jnp
from jax import lax
from jax.experimental import pallas as pl
from jax.experimental.pallas import tpu as pltpu

D_MODEL = 1024
N_MEM = 256
CONV_CH = 512
CONV_K = 3
HEAD_DIM = 64
N_Q_HEADS = 8
N_KV_HEADS = 2
ATTN_W = N_Q_HEADS * HEAD_DIM
KV_W = N_KV_HEADS * HEAD_DIM
WINDOW = 128
ROPE_THETA = 10000.0
N_X_HEADS = 4
X_HEAD_DIM = D_MODEL // N_X_HEADS
D_FF = 4 * D_MODEL
EPS = 1e-6
LOG2E = math.log2(math.e)
IN_COLS = 3 * CONV_CH + ATTN_W + 2 * KV_W
Q_OFF = 3 * CONV_CH
K_OFF = Q_OFF + ATTN_W

LANES = 128
SUBLANES = 8
VMEM_LIMIT_BYTES = 56 * 1024 * 1024

SEQ_TILE = 512
SUB_TILE = 256
MLP_TILE = 512
FF_CHUNK = 1024
CARRY_ROWS = SUBLANES

BF16 = jnp.bfloat16
F32 = jnp.float32


def _rms(x, g):
    ms = jnp.mean(x * x, axis=-1, keepdims=True)
    return x * lax.rsqrt(ms + EPS) * g


def _dot(a, b):
    return jnp.dot(a, b.astype(BF16), preferred_element_type=F32)


def _resident(shape, index_map):
    return pl.BlockSpec(shape, index_map, pipeline_mode=pl.Buffered(1))


def _rope_kernel(pos_ref, inv_ref, cos_ref, sin_ref):
    ang = inv_ref[...] * pos_ref[0].astype(F32)
    c = jnp.cos(ang)
    s = jnp.sin(ang)
    cos_ref[0] = jnp.concatenate([c, c, c, c], axis=0).T
    sin_ref[0] = jnp.concatenate([-s, s, -s, s], axis=0).T


def _rope_tables(positions):
    b, s = positions.shape
    half = HEAD_DIM // 2
    inv_freq = ROPE_THETA ** (-jnp.arange(0, HEAD_DIM, 2, dtype=F32) / HEAD_DIM)
    table = pl.BlockSpec((1, s, LANES), lambda i: (i, 0, 0))
    return pl.pallas_call(
        _rope_kernel,
        grid=(b,),
        in_specs=[pl.BlockSpec((1, 1, s), lambda i: (i, 0, 0)),
                  pl.BlockSpec((half, 1), lambda i: (0, 0))],
        out_specs=[table, table],
        out_shape=[jax.ShapeDtypeStruct((b, s, LANES), F32)] * 2,
        name="rope_tables",
    )(positions.reshape(b, 1, s), inv_freq.reshape(half, 1))


def _mix_kernel(sinks_ref, x_ref, cos_ref, sin_ref, gmix_ref, win_ref, convw_ref,
                gconv_ref, gattn_ref, wout_ref, o_ref,
                ke0, ko0, ke1, ko1, ve0, vo0, ve1, vo1, cbuf, qbuf, abuf, *, layer):
    ts = SEQ_TILE
    kbufs = ((ke0, ko0), (ke1, ko1))
    vbufs = ((ve0, vo0), (ve1, vo1))
    kv_all = (ke0, ko0, ke1, ko1, ve0, vo0, ve1, vo1)
    seq_start = pl.program_id(1) == 0

    @pl.when(seq_start)
    def _():
        for buf in kv_all:
            buf[0:WINDOW, :] = jnp.zeros((WINDOW, LANES), BF16)
        cbuf[0:CARRY_ROWS, :] = jnp.zeros((CARRY_ROWS, CONV_CH), F32)

    half_dim = HEAD_DIM // 2
    scale = LOG2E / math.sqrt(HEAD_DIM)
    w = convw_ref[...]

    qi = lax.broadcasted_iota(jnp.int32, (WINDOW, WINDOW), 0)
    ki = lax.broadcasted_iota(jnp.int32, (WINDOW, WINDOW), 1)
    own = ki <= qi
    no_prev = ki < WINDOW * seq_start.astype(jnp.int32)

    def chain(r0):
        n = SUB_TILE
        x = x_ref[0, r0:r0 + n, :]
        h = _rms(x, gmix_ref[...]).astype(BF16)

        cosf = cos_ref[0, r0:r0 + n, :]
        lane = lax.broadcasted_iota(jnp.int32, (n, LANES), 1)
        first_half = (lane & half_dim) == 0
        low_head = lane < HEAD_DIM
        sins = sin_ref[0, r0:r0 + n, :]

        def rope(t):
            rot = jnp.where(first_half,
                            pltpu.roll(t, LANES - half_dim, 1),
                            pltpu.roll(t, half_dim, 1))
            return t * cosf + rot * sins

        uq = _dot(h, win_ref[:, Q_OFF:K_OFF])
        for c in range(ATTN_W // LANES):
            qc = (rope(uq[:, c * LANES:(c + 1) * LANES]) * scale).astype(BF16)
            for jb in range(n // WINDOW):
                qbuf[r0 // WINDOW + jb, c // 2, (c % 2) * WINDOW:(c % 2 + 1) * WINDOW, :] = (
                    qc[jb * WINDOW:(jb + 1) * WINDOW])

        ukv = _dot(h, win_ref[:, K_OFF:IN_COLS])
        k = rope(ukv[:, 0:KV_W])
        v = ukv[:, KV_W:2 * KV_W]
        k0 = WINDOW + r0
        for src, bufs in ((k, kbufs), (v, vbufs)):
            e0 = jnp.where(low_head, src, 0.0)
            o1 = jnp.where(low_head, 0.0, src)
            bufs[0][0][k0:k0 + n, :] = e0.astype(BF16)
            bufs[0][1][k0:k0 + n, :] = pltpu.roll(e0, HEAD_DIM, 1).astype(BF16)
            bufs[1][0][k0:k0 + n, :] = pltpu.roll(o1, HEAD_DIM, 1).astype(BF16)
            bufs[1][1][k0:k0 + n, :] = o1.astype(BF16)

        uc = _dot(h, win_ref[:, 0:Q_OFF])
        cx = uc[:, CONV_CH:2 * CONV_CH] * uc[:, 2 * CONV_CH:3 * CONV_CH]
        c0 = CARRY_ROWS + r0
        cbuf[c0:c0 + n, :] = cx
        y = (w[2:3, :] * cx
             + w[1:2, :] * cbuf[c0 - 1:c0 - 1 + n, :]
             + w[0:1, :] * cbuf[c0 - 2:c0 - 2 + n, :])
        conv_n = _rms(uc[:, 0:CONV_CH] * y, gconv_ref[...]).astype(BF16)

        for jb in range(n // WINDOW):
            j = r0 // WINDOW + jb
            w0 = j * WINDOW
            for g in range(N_KV_HEADS):
                qg = qbuf[j, g]
                o = None
                for par in range(2):
                    s = lax.dot_general(qg, kbufs[g][par][w0:w0 + 2 * WINDOW, :],
                                        (((1,), (1,)), ((), ())),
                                        preferred_element_type=F32)
                    ps, inv_l = [], []
                    for sub in range(2):
                        sink = sinks_ref[layer, 4 * g + 2 * sub + par] * LOG2E
                        s_prev = s[sub * WINDOW:(sub + 1) * WINDOW, 0:WINDOW]
                        s_own = s[sub * WINDOW:(sub + 1) * WINDOW, WINDOW:2 * WINDOW]
                        if j == 0:
                            s_prev = jnp.where(no_prev, -jnp.inf, s_prev)
                        sh = jnp.where(own, s_own, s_prev)
                        m = jnp.maximum(jnp.max(sh, axis=-1, keepdims=True), sink)
                        p = jnp.exp2(sh - m)
                        l = jnp.sum(p, axis=-1, keepdims=True) + jnp.exp2(sink - m)
                        p_own = jnp.where(own, p, 0.0)
                        ps.append(jnp.concatenate([p - p_own, p_own], axis=1).astype(BF16))
                        inv_l.append(1.0 / l)
                    pv = _dot(jnp.concatenate(ps, axis=0), vbufs[g][par][w0:w0 + 2 * WINDOW, :])
                    o_par = [pv[sub * WINDOW:(sub + 1) * WINDOW] * inv_l[sub] for sub in range(2)]
                    o = o_par if o is None else [a + b for a, b in zip(o, o_par)]
                for sub in range(2):
                    c = 2 * g + sub
                    abuf[j * WINDOW:(j + 1) * WINDOW, c * LANES:(c + 1) * LANES] = o[sub]

        attn_n = _rms(abuf[r0:r0 + n, :], gattn_ref[...]).astype(BF16)
        o_ref[0, r0:r0 + n, :] = (x + _dot(conv_n, wout_ref[0:CONV_CH, :])
                                  + _dot(attn_n, wout_ref[CONV_CH:CONV_CH + ATTN_W, :]))

    for r0 in range(0, ts, SUB_TILE):
        chain(r0)

    cbuf[0:CARRY_ROWS, :] = cbuf[ts:ts + CARRY_ROWS, :]
    for buf in kv_all:
        buf[0:WINDOW, :] = buf[ts:ts + WINDOW, :]


def _mix_layer(layer, x, cos, sin, sinks, gmix, w_in, conv_w, gconv, gattn, w_out):
    b, s, d = x.shape
    ts = SEQ_TILE
    tok = lambda w: pl.BlockSpec((1, ts, w), lambda i, j: (i, j, 0))
    per_layer = lambda *shape: _resident((None,) + shape, lambda i, j: (layer,) + (0,) * len(shape))
    kv_scratch = [pltpu.VMEM((ts + WINDOW, LANES), BF16)] * 8
    return pl.pallas_call(
        functools.partial(_mix_kernel, layer=layer),
        grid=(b, s // ts),
        in_specs=[pl.BlockSpec(memory_space=pltpu.SMEM),
                  tok(d), tok(LANES), tok(LANES),
                  per_layer(1, d), per_layer(d, IN_COLS), per_layer(CONV_K, CONV_CH),
                  per_layer(1, CONV_CH), per_layer(1, ATTN_W), per_layer(CONV_CH + ATTN_W, d)],
        out_specs=tok(d),
        out_shape=jax.ShapeDtypeStruct((b, s, d), F32),
        scratch_shapes=kv_scratch + [
            pltpu.VMEM((ts + CARRY_ROWS, CONV_CH), F32),
            pltpu.VMEM((ts // WINDOW, N_KV_HEADS, 2 * WINDOW, LANES), BF16),
            pltpu.VMEM((ts, ATTN_W), F32)],
        compiler_params=pltpu.CompilerParams(
            dimension_semantics=("arbitrary", "arbitrary"),
            vmem_limit_bytes=VMEM_LIMIT_BYTES),
        name="token_mix",
    )(sinks, x, cos, sin, gmix, w_in, conv_w, gconv, gattn, w_out)


def _memkv_kernel(mem_ref, g_ref, wkv_ref, kt_ref, v_ref):
    memn = _rms(mem_ref[0], g_ref[...]).astype(BF16)
    kv = _dot(memn, wkv_ref[...])
    kt_ref[...] = kv[:, 0:D_MODEL].T.astype(BF16)
    v_ref[...] = kv[:, D_MODEL:2 * D_MODEL].astype(BF16)


def _mem_kv(mem, norm_mem_g, wx_kv):
    b, m, d = mem.shape
    depth = wx_kv.shape[0]
    return pl.pallas_call(
        _memkv_kernel,
        grid=(depth, b),
        in_specs=[pl.BlockSpec((1, m, d), lambda l, i: (i, 0, 0)),
                  pl.BlockSpec((None, 1, d), lambda l, i: (l, 0, 0)),
                  pl.BlockSpec((None, d, 2 * d), lambda l, i: (l, 0, 0))],
        out_specs=[pl.BlockSpec((None, None, d, m), lambda l, i: (l, i, 0, 0)),
                   pl.BlockSpec((None, None, m, d), lambda l, i: (l, i, 0, 0))],
        out_shape=[jax.ShapeDtypeStruct((depth, b, d, m), BF16),
                   jax.ShapeDtypeStruct((depth, b, m, d), BF16)],
        compiler_params=pltpu.CompilerParams(vmem_limit_bytes=VMEM_LIMIT_BYTES),
        name="mem_kv",
    )(mem, norm_mem_g, wx_kv)


def _xattn_kernel(x_ref, g_ref, wq_ref, kt_ref, v_ref, wo_ref, o_ref):
    x = x_ref[0]
    h = _rms(x, g_ref[...]).astype(BF16)
    scale = 1.0 / math.sqrt(X_HEAD_DIM)
    q = (_dot(h, wq_ref[...]) * scale).astype(BF16)
    outs = []
    for hd in range(N_X_HEADS):
        sl = slice(hd * X_HEAD_DIM, (hd + 1) * X_HEAD_DIM)
        s = _dot(q[:, sl], kt_ref[sl, :])
        m = jnp.max(s, axis=-1, keepdims=True)
        p = jnp.exp(s - m)
        l = jnp.sum(p, axis=-1, keepdims=True)
        o = _dot(p.astype(BF16), v_ref[:, sl]) * (1.0 / l)
        outs.append(o.astype(BF16))
    o_ref[0] = x + _dot(jnp.concatenate(outs, axis=1), wo_ref[...])


def _xattn_layer(layer, x, g, wq, kt, v, wo):
    b, s, d = x.shape
    ts = SEQ_TILE
    tok = pl.BlockSpec((1, ts, d), lambda i, j: (i, j, 0))
    per_layer = lambda *shape: _resident((None,) + shape, lambda i, j: (layer,) + (0,) * len(shape))
    return pl.pallas_call(
        _xattn_kernel,
        grid=(b, s // ts),
        in_specs=[tok, per_layer(1, d), per_layer(d, d),
                  pl.BlockSpec((None, None, d, N_MEM), lambda i, j: (layer, i, 0, 0)),
                  pl.BlockSpec((None, None, N_MEM, d), lambda i, j: (layer, i, 0, 0)),
                  per_layer(d, d)],
        out_specs=tok,
        out_shape=jax.ShapeDtypeStruct((b, s, d), F32),
        compiler_params=pltpu.CompilerParams(vmem_limit_bytes=VMEM_LIMIT_BYTES),
        name="mem_xattn",
    )(x, g, wq, kt, v, wo)


def _mlp_kernel(x_ref, g_ref, wup_ref, wdn_ref, *rest, final):
    o_ref = rest[-1]
    x = x_ref[...]
    h = _rms(x, g_ref[...]).astype(BF16)
    acc = x
    for c in range(D_FF // FF_CHUNK):
        sl = slice(c * FF_CHUNK, (c + 1) * FF_CHUNK)
        up = _dot(h, wup_ref[:, sl])
        act = jnp.square(jnp.maximum(up, 0.0)).astype(BF16)
        acc = acc + _dot(act, wdn_ref[sl, :])
    if final:
        acc = _rms(acc, rest[0][...])
    o_ref[...] = acc


def _mlp_layer(layer, x2d, g, w_up, w_down, final_g):
    n, d = x2d.shape
    tm = MLP_TILE
    final = final_g is not None
    tok = pl.BlockSpec((tm, d), lambda i: (i, 0))
    per_layer = lambda *shape: _resident((None,) + shape, lambda i: (layer,) + (0,) * len(shape))
    in_specs = [tok, per_layer(1, d), per_layer(d, D_FF), per_layer(D_FF, d)]
    args = [x2d, g, w_up, w_down]
    if final:
        in_specs.append(_resident((1, d), lambda i: (0, 0)))
        args.append(final_g.reshape(1, d))
    return pl.pallas_call(
        functools.partial(_mlp_kernel, final=final),
        grid=(n // tm,),
        in_specs=in_specs,
        out_specs=tok,
        out_shape=jax.ShapeDtypeStruct((n, d), F32),
        compiler_params=pltpu.CompilerParams(vmem_limit_bytes=VMEM_LIMIT_BYTES),
        name="relu2_mlp",
    )(*args)


def kernel(x, mem, positions, norm_mix_g, w_in, conv_w, sinks, gnorm_conv_g, gnorm_attn_g, w_out, norm_x_g, norm_mem_g, wx_q, wx_kv, wx_o, norm_mlp_g, w_up, w_down, final_g):
    b, s, d = x.shape
    depth = w_in.shape[0]
    row = lambda g: g.reshape(depth, 1, g.shape[-1])
    cos, sin = _rope_tables(positions)
    kt, v = _mem_kv(mem, row(norm_mem_g), wx_kv)
    for l in range(depth):
        x = _mix_layer(l, x, cos, sin, sinks, row(norm_mix_g), w_in, conv_w,
                       row(gnorm_conv_g), row(gnorm_attn_g), w_out)
        x = _xattn_layer(l, x, row(norm_x_g), wx_q, kt, v, wx_o)
        x = _mlp_layer(l, x.reshape(b * s, d), row(norm_mlp_g), w_up, w_down,
                       final_g if l == depth - 1 else None).reshape(b, s, d)
    return x
```

```python
import functools
import math

import jax
import jax.numpy as jnp
from jax import lax
from jax.experimental import pallas as pl
from jax.experimental.pallas import tpu as pltpu

D_MODEL = 1024
N_MEM = 256
CONV_CH = 512
CONV_K = 3
HEAD_DIM = 64
N_Q_HEADS = 8
N_KV_HEADS = 2
ATTN_W = N_Q_HEADS * HEAD_DIM
KV_W = N_KV_HEADS * HEAD_DIM
WINDOW = 128
ROPE_THETA = 10000.0
N_X_HEADS = 4
X_HEAD_DIM = D_MODEL // N_X_HEADS
D_FF = 4 * D_MODEL
EPS = 1e-6
LOG2E = math.log2(math.e)
IN_COLS = 3 * CONV_CH + ATTN_W + 2 * KV_W
Q_OFF = 3 * CONV_CH
K_OFF = Q_OFF + ATTN_W

LANES = 128
SUBLANES = 8
VMEM_LIMIT_BYTES = 56 * 1024 * 1024

SEQ_TILE = 1024
SUB_TILE = 256
MLP_TILE = 512
FF_CHUNK = 1024
CARRY_ROWS = SUBLANES

BF16 = jnp.bfloat16
F32 = jnp.float32


def _rms(x, g):
    ms = jnp.mean(x * x, axis=-1, keepdims=True)
    return x * lax.rsqrt(ms + EPS) * g


def _dot(a, b):
    return jnp.dot(a, b.astype(BF16), preferred_element_type=F32)


def _resident(shape, index_map):
    return pl.BlockSpec(shape, index_map, pipeline_mode=pl.Buffered(1))


def _rope_kernel(pos_ref, inv_ref, cos_ref, sin_ref):
    ang = inv_ref[...] * pos_ref[0].astype(F32)
    c = jnp.cos(ang)
    s = jnp.sin(ang)
    cos_ref[0] = jnp.concatenate([c, c, c, c], axis=0).T
    sin_ref[0] = jnp.concatenate([-s, s, -s, s], axis=0).T


def _rope_tables(positions):
    b, s = positions.shape
    half = HEAD_DIM // 2
    inv_freq = ROPE_THETA ** (-jnp.arange(0, HEAD_DIM, 2, dtype=F32) / HEAD_DIM)
    table = pl.BlockSpec((1, s, LANES), lambda i: (i, 0, 0))
    return pl.pallas_call(
        _rope_kernel,
        grid=(b,),
        in_specs=[pl.BlockSpec((1, 1, s), lambda i: (i, 0, 0)),
                  pl.BlockSpec((half, 1), lambda i: (0, 0))],
        out_specs=[table, table],
        out_shape=[jax.ShapeDtypeStruct((b, s, LANES), F32)] * 2,
        name="rope_tables",
    )(positions.reshape(b, 1, s), inv_freq.reshape(half, 1))


def _mix_kernel(sinks_ref, x_ref, cos_ref, sin_ref, gmix_ref, win_ref, convw_ref,
                gconv_ref, gattn_ref, wout_ref, o_ref,
                ke0, ko0, ke1, ko1, ve0, vo0, ve1, vo1, cbuf, qbuf, abuf, *, layer):
    ts = SEQ_TILE
    kbufs = ((ke0, ko0), (ke1, ko1))
    vbufs = ((ve0, vo0), (ve1, vo1))
    kv_all = (ke0, ko0, ke1, ko1, ve0, vo0, ve1, vo1)
    seq_start = pl.program_id(1) == 0

    @pl.when(seq_start)
    def _():
        for buf in kv_all:
            buf[0:WINDOW, :] = jnp.zeros((WINDOW, LANES), BF16)
        cbuf[0:CARRY_ROWS, :] = jnp.zeros((CARRY_ROWS, CONV_CH), F32)

    half_dim = HEAD_DIM // 2
    scale = LOG2E / math.sqrt(HEAD_DIM)
    w = convw_ref[...]

    qi = lax.broadcasted_iota(jnp.int32, (WINDOW, WINDOW), 0)
    ki = lax.broadcasted_iota(jnp.int32, (WINDOW, WINDOW), 1)
    own = ki <= qi
    no_prev = ki < WINDOW * seq_start.astype(jnp.int32)

    def chain(r0):
        n = SUB_TILE
        x = x_ref[0, r0:r0 + n, :]
        h = _rms(x, gmix_ref[...]).astype(BF16)

        cosf = cos_ref[0, r0:r0 + n, :]
        lane = lax.broadcasted_iota(jnp.int32, (n, LANES), 1)
        first_half = (lane & half_dim) == 0
        low_head = lane < HEAD_DIM
        sins = sin_ref[0, r0:r0 + n, :]

        def rope(t):
            rot = jnp.where(first_half,
                            pltpu.roll(t, LANES - half_dim, 1),
                            pltpu.roll(t, half_dim, 1))
            return t * cosf + rot * sins

        uq = _dot(h, win_ref[:, Q_OFF:K_OFF])
        for c in range(ATTN_W // LANES):
            qc = (rope(uq[:, c * LANES:(c + 1) * LANES]) * scale).astype(BF16)
            for jb in range(n // WINDOW):
                qbuf[r0 // WINDOW + jb, c // 2, (c % 2) * WINDOW:(c % 2 + 1) * WINDOW, :] = (
                    qc[jb * WINDOW:(jb + 1) * WINDOW])

        ukv = _dot(h, win_ref[:, K_OFF:IN_COLS])
        k = rope(ukv[:, 0:KV_W])
        v = ukv[:, KV_W:2 * KV_W]
        k0 = WINDOW + r0
        for src, bufs in ((k, kbufs), (v, vbufs)):
            e0 = jnp.where(low_head, src, 0.0)
            o1 = jnp.where(low_head, 0.0, src)
            bufs[0][0][k0:k0 + n, :] = e0.astype(BF16)
            bufs[0][1][k0:k0 + n, :] = pltpu.roll(e0, HEAD_DIM, 1).astype(BF16)
            bufs[1][0][k0:k0 + n, :] = pltpu.roll(o1, HEAD_DIM, 1).astype(BF16)
            bufs[1][1][k0:k0 + n, :] = o1.astype(BF16)

        uc = _dot(h, win_ref[:, 0:Q_OFF])
        cx = uc[:, CONV_CH:2 * CONV_CH] * uc[:, 2 * CONV_CH:3 * CONV_CH]
        c0 = CARRY_ROWS + r0
        cbuf[c0:c0 + n, :] = cx
        y = (w[2:3, :] * cx
             + w[1:2, :] * cbuf[c0 - 1:c0 - 1 + n, :]
             + w[0:1, :] * cbuf[c0 - 2:c0 - 2 + n, :])
        conv_n = _rms(uc[:, 0:CONV_CH] * y, gconv_ref[...]).astype(BF16)

        for jb in range(n // WINDOW):
            j = r0 // WINDOW + jb
            w0 = j * WINDOW
            for g in range(N_KV_HEADS):
                qg = qbuf[j, g]
                o = None
                for par in range(2):
                    s = lax.dot_general(qg, kbufs[g][par][w0:w0 + 2 * WINDOW, :],
                                        (((1,), (1,)), ((), ())),
                                        preferred_element_type=F32)
                    ps, inv_l = [], []
                    for sub in range(2):
                        sink = sinks_ref[layer, 4 * g + 2 * sub + par] * LOG2E
                        s_prev = s[sub * WINDOW:(sub + 1) * WINDOW, 0:WINDOW]
                        s_own = s[sub * WINDOW:(sub + 1) * WINDOW, WINDOW:2 * WINDOW]
                        if j == 0:
                            s_prev = jnp.where(no_prev, -jnp.inf, s_prev)
                        sh = jnp.where(own, s_own, s_prev)
                        m = jnp.maximum(jnp.max(sh, axis=-1, keepdims=True), sink)
                        p = jnp.exp2(sh - m)
                        l = jnp.sum(p, axis=-1, keepdims=True) + jnp.exp2(sink - m)
                        p_own = jnp.where(own, p, 0.0)
                        ps.append(jnp.concatenate([p - p_own, p_own], axis=1).astype(BF16))
                        inv_l.append(1.0 / l)
                    pv = _dot(jnp.concatenate(ps, axis=0), vbufs[g][par][w0:w0 + 2 * WINDOW, :])
                    o_par = [pv[sub * WINDOW:(sub + 1) * WINDOW] * inv_l[sub] for sub in range(2)]
                    o = o_par if o is None else [a + b for a, b in zip(o, o_par)]
                for sub in range(2):
                    c = 2 * g + sub
                    abuf[j * WINDOW:(j + 1) * WINDOW, c * LANES:(c + 1) * LANES] = o[sub]

        attn_n = _rms(abuf[r0:r0 + n, :], gattn_ref[...]).astype(BF16)
        o_ref[0, r0:r0 + n, :] = (x + _dot(conv_n, wout_ref[0:CONV_CH, :])
                                  + _dot(attn_n, wout_ref[CONV_CH:CONV_CH + ATTN_W, :]))

    for r0 in range(0, ts, SUB_TILE):
        chain(r0)

    cbuf[0:CARRY_ROWS, :] = cbuf[ts:ts + CARRY_ROWS, :]
    for buf in kv_all:
        buf[0:WINDOW, :] = buf[ts:ts + WINDOW, :]


def _mix_layer(layer, x, cos, sin, sinks, gmix, w_in, conv_w, gconv, gattn, w_out):
    b, s, d = x.shape
    ts = SEQ_TILE
    tok = lambda w: pl.BlockSpec((1, ts, w), lambda i, j: (i, j, 0))
    per_layer = lambda *shape: _resident((None,) + shape, lambda i, j: (layer,) + (0,) * len(shape))
    kv_scratch = [pltpu.VMEM((ts + WINDOW, LANES), BF16)] * 8
    return pl.pallas_call(
        functools.partial(_mix_kernel, layer=layer),
        grid=(b, s // ts),
        in_specs=[pl.BlockSpec(memory_space=pltpu.SMEM),
                  tok(d), tok(LANES), tok(LANES),
                  per_layer(1, d), per_layer(d, IN_COLS), per_layer(CONV_K, CONV_CH),
                  per_layer(1, CONV_CH), per_layer(1, ATTN_W), per_layer(CONV_CH + ATTN_W, d)],
        out_specs=tok(d),
        out_shape=jax.ShapeDtypeStruct((b, s, d), F32),
        scratch_shapes=kv_scratch + [
            pltpu.VMEM((ts + CARRY_ROWS, CONV_CH), F32),
            pltpu.VMEM((ts // WINDOW, N_KV_HEADS, 2 * WINDOW, LANES), BF16),
            pltpu.VMEM((ts, ATTN_W), F32)],
        compiler_params=pltpu.CompilerParams(
            dimension_semantics=("arbitrary", "arbitrary"),
            vmem_limit_bytes=VMEM_LIMIT_BYTES),
        name="token_mix",
    )(sinks, x, cos, sin, gmix, w_in, conv_w, gconv, gattn, w_out)


def _memkv_kernel(mem_ref, g_ref, wkv_ref, kt_ref, v_ref):
    memn = _rms(mem_ref[0], g_ref[...]).astype(BF16)
    kv = _dot(memn, wkv_ref[...])
    kt_ref[...] = kv[:, 0:D_MODEL].T.astype(BF16)
    v_ref[...] = kv[:, D_MODEL:2 * D_MODEL].astype(BF16)


def _mem_kv(mem, norm_mem_g, wx_kv):
    b, m, d = mem.shape
    depth = wx_kv.shape[0]
    return pl.pallas_call(
        _memkv_kernel,
        grid=(depth, b),
        in_specs=[pl.BlockSpec((1, m, d), lambda l, i: (i, 0, 0)),
                  pl.BlockSpec((None, 1, d), lambda l, i: (l, 0, 0)),
                  pl.BlockSpec((None, d, 2 * d), lambda l, i: (l, 0, 0))],
        out_specs=[pl.BlockSpec((None, None, d, m), lambda l, i: (l, i, 0, 0)),
                   pl.BlockSpec((None, None, m, d), lambda l, i: (l, i, 0, 0))],
        out_shape=[jax.ShapeDtypeStruct((depth, b, d, m), BF16),
                   jax.ShapeDtypeStruct((depth, b, m, d), BF16)],
        compiler_params=pltpu.CompilerParams(vmem_limit_bytes=VMEM_LIMIT_BYTES),
        name="mem_kv",
    )(mem, norm_mem_g, wx_kv)


def _xattn_kernel(x_ref, g_ref, wq_ref, kt_ref, v_ref, wo_ref, o_ref):
    x = x_ref[0]
    h = _rms(x, g_ref[...]).astype(BF16)
    scale = 1.0 / math.sqrt(X_HEAD_DIM)
    q = (_dot(h, wq_ref[...]) * scale).astype(BF16)
    outs = []
    for hd in range(N_X_HEADS):
        sl = slice(hd * X_HEAD_DIM, (hd + 1) * X_HEAD_DIM)
        s = _dot(q[:, sl], kt_ref[sl, :])
        m = jnp.max(s, axis=-1, keepdims=True)
        p = jnp.exp(s - m)
        l = jnp.sum(p, axis=-1, keepdims=True)
        o = _dot(p.astype(BF16), v_ref[:, sl]) * (1.0 / l)
        outs.append(o.astype(BF16))
    o_ref[0] = x + _dot(jnp.concatenate(outs, axis=1), wo_ref[...])


def _xattn_layer(layer, x, g, wq, kt, v, wo):
    b, s, d = x.shape
    ts = SEQ_TILE
    tok = pl.BlockSpec((1, ts, d), lambda i, j: (i, j, 0))
    per_layer = lambda *shape: _resident((None,) + shape, lambda i, j: (layer,) + (0,) * len(shape))
    return pl.pallas_call(
        _xattn_kernel,
        grid=(b, s // ts),
        in_specs=[tok, per_layer(1, d), per_layer(d, d),
                  pl.BlockSpec((None, None, d, N_MEM), lambda i, j: (layer, i, 0, 0)),
                  pl.BlockSpec((None, None, N_MEM, d), lambda i, j: (layer, i, 0, 0)),
                  per_layer(d, d)],
        out_specs=tok,
        out_shape=jax.ShapeDtypeStruct((b, s, d), F32),
        compiler_params=pltpu.CompilerParams(vmem_limit_bytes=VMEM_LIMIT_BYTES),
        name="mem_xattn",
    )(x, g, wq, kt, v, wo)


def _mlp_kernel(x_ref, g_ref, wup_ref, wdn_ref, *rest, final):
    o_ref = rest[-1]
    x = x_ref[...]
    h = _rms(x, g_ref[...]).astype(BF16)
    acc = x
    for c in range(D_FF // FF_CHUNK):
        sl = slice(c * FF_CHUNK, (c + 1) * FF_CHUNK)
        up = _dot(h, wup_ref[:, sl])
        act = jnp.square(jnp.maximum(up, 0.0)).astype(BF16)
        acc = acc + _dot(act, wdn_ref[sl, :])
    if final:
        acc = _rms(acc, rest[0][...])
    o_ref[...] = acc


def _mlp_layer(layer, x2d, g, w_up, w_down, final_g):
    n, d = x2d.shape
    tm = MLP_TILE
    final = final_g is not None
    tok = pl.BlockSpec((tm, d), lambda i: (i, 0))
    per_layer = lambda *shape: _resident((None,) + shape, lambda i: (layer,) + (0,) * len(shape))
    in_specs = [tok, per_layer(1, d), per_layer(d, D_FF), per_layer(D_FF, d)]
    args = [x2d, g, w_up, w_down]
    if final:
        in_specs.append(_resident((1, d), lambda i: (0, 0)))
        args.append(final_g.reshape(1, d))
    return pl.pallas_call(
        functools.partial(_mlp_kernel, final=final),
        grid=(n // tm,),
        in_specs=in_specs,
        out_specs=tok,
        out_shape=jax.ShapeDtypeStruct((n, d), F32),
        compiler_params=pltpu.CompilerParams(vmem_limit_bytes=VMEM_LIMIT_BYTES),
        name="relu2_mlp",
    )(*args)


def kernel(x, mem, positions, norm_mix_g, w_in, conv_w, sinks, gnorm_conv_g, gnorm_attn_g, w_out, norm_x_g, norm_mem_g, wx_q, wx_kv, wx_o, norm_mlp_g, w_up, w_down, final_g):
    b, s, d = x.shape
    depth = w_in.shape[0]
    row = lambda g: g.reshape(depth, 1, g.shape[-1])
    cos, sin = _rope_tables(positions)
    kt, v = _mem_kv(mem, row(norm_mem_g), wx_kv)
    for l in range(depth):
        x = _mix_layer(l, x, cos, sin, sinks, row(norm_mix_g), w_in, conv_w,
                       row(gnorm_conv_g), row(gnorm_attn_g), w_out)
        x = _xattn_layer(l, x, row(norm_x_g), wx_q, kt, v, wx_o)
        x = _mlp_layer(l, x.reshape(b * s, d), row(norm_mlp_g), w_up, w_down,
                       final_g if l == depth - 1 else None).reshape(b, s, d)
    return x
```

```python
import functools
import math

import jax
import jax.numpy as jnp
from jax import lax
from jax.experimental import pallas as pl
from jax.experimental.pallas import tpu as pltpu

D_MODEL = 1024
N_MEM = 256
CONV_CH = 512
CONV_K = 3
HEAD_DIM = 64
N_Q_HEADS = 8
N_KV_HEADS = 2
ATTN_W = N_Q_HEADS * HEAD_DIM
KV_W = N_KV_HEADS * HEAD_DIM
WINDOW = 128
ROPE_THETA = 10000.0
N_X_HEADS = 4
X_HEAD_DIM = D_MODEL // N_X_HEADS
D_FF = 4 * D_MODEL
EPS = 1e-6
LOG2E = math.log2(math.e)
IN_COLS = 3 * CONV_CH + ATTN_W + 2 * KV_W
Q_OFF = 3 * CONV_CH
K_OFF = Q_OFF + ATTN_W

LANES = 128
SUBLANES = 8
VMEM_LIMIT_BYTES = 56 * 1024 * 1024

SEQ_TILE = 1024
SUB_TILE = 256
MLP_TILE = 512
FF_CHUNK = 1024
CARRY_ROWS = SUBLANES

BF16 = jnp.bfloat16
F32 = jnp.float32


def _rms(x, g):
    ms = jnp.mean(x * x, axis=-1, keepdims=True)
    return x * lax.rsqrt(ms + EPS) * g


def _dot(a, b):
    return jnp.dot(a, b.astype(BF16), preferred_element_type=F32)


def _resident(shape, index_map):
    return pl.BlockSpec(shape, index_map, pipeline_mode=pl.Buffered(1))


def _rope_kernel(pos_ref, inv_ref, cos_ref, sin_ref):
    ang = inv_ref[...] * pos_ref[0].astype(F32)
    c = jnp.cos(ang)
    s = jnp.sin(ang)
    cos_ref[0] = jnp.concatenate([c, c, c, c], axis=0).T
    sin_ref[0] = jnp.concatenate([-s, s, -s, s], axis=0).T


def _rope_tables(positions):
    b, s = positions.shape
    half = HEAD_DIM // 2
    inv_freq = ROPE_THETA ** (-jnp.arange(0, HEAD_DIM, 2, dtype=F32) / HEAD_DIM)
    table = pl.BlockSpec((1, s, LANES), lambda i: (i, 0, 0))
    return pl.pallas_call(
        _rope_kernel,
        grid=(b,),
        in_specs=[pl.BlockSpec((1, 1, s), lambda i: (i, 0, 0)),
                  pl.BlockSpec((half, 1), lambda i: (0, 0))],
        out_specs=[table, table],
        out_shape=[jax.ShapeDtypeStruct((b, s, LANES), F32)] * 2,
        name="rope_tables",
    )(positions.reshape(b, 1, s), inv_freq.reshape(half, 1))


def _mix_kernel(sinks_ref, x_ref, cos_ref, sin_ref, gmix_ref, win_ref, convw_ref,
                gconv_ref, gattn_ref, wout_ref, o_ref,
                ke0, ko0, ke1, ko1, ve0, vo0, ve1, vo1, cbuf, qbuf, abuf, *, layer):
    ts = SEQ_TILE
    kbufs = ((ke0, ko0), (ke1, ko1))
    vbufs = ((ve0, vo0), (ve1, vo1))
    kv_all = (ke0, ko0, ke1, ko1, ve0, vo0, ve1, vo1)
    seq_start = pl.program_id(1) == 0

    @pl.when(seq_start)
    def _():
        for buf in kv_all:
            buf[0:WINDOW, :] = jnp.zeros((WINDOW, LANES), BF16)
        cbuf[0:CARRY_ROWS, :] = jnp.zeros((CARRY_ROWS, CONV_CH), F32)

    half_dim = HEAD_DIM // 2
    scale = LOG2E / math.sqrt(HEAD_DIM)
    w = convw_ref[...]

    qi = lax.broadcasted_iota(jnp.int32, (WINDOW, WINDOW), 0)
    ki = lax.broadcasted_iota(jnp.int32, (WINDOW, WINDOW), 1)
    own = ki <= qi
    no_prev = ki < WINDOW * seq_start.astype(jnp.int32)

    carried = {}

    def front(r0):
        n = SUB_TILE
        x = x_ref[0, r0:r0 + n, :]
        h = _rms(x, gmix_ref[...]).astype(BF16)

        cosf = cos_ref[0, r0:r0 + n, :]
        lane = lax.broadcasted_iota(jnp.int32, (n, LANES), 1)
        first_half = (lane & half_dim) == 0
        low_head = lane < HEAD_DIM
        sins = sin_ref[0, r0:r0 + n, :]

        def rope(t):
            rot = jnp.where(first_half,
                            pltpu.roll(t, LANES - half_dim, 1),
                            pltpu.roll(t, half_dim, 1))
            return t * cosf + rot * sins

        uq = _dot(h, win_ref[:, Q_OFF:K_OFF])
        for c in range(ATTN_W // LANES):
            qc = (rope(uq[:, c * LANES:(c + 1) * LANES]) * scale).astype(BF16)
            for jb in range(n // WINDOW):
                qbuf[r0 // WINDOW + jb, c // 2, (c % 2) * WINDOW:(c % 2 + 1) * WINDOW, :] = (
                    qc[jb * WINDOW:(jb + 1) * WINDOW])

        ukv = _dot(h, win_ref[:, K_OFF:IN_COLS])
        k = rope(ukv[:, 0:KV_W])
        v = ukv[:, KV_W:2 * KV_W]
        k0 = WINDOW + r0
        for src, bufs in ((k, kbufs), (v, vbufs)):
            e0 = jnp.where(low_head, src, 0.0)
            o1 = jnp.where(low_head, 0.0, src)
            bufs[0][0][k0:k0 + n, :] = e0.astype(BF16)
            bufs[0][1][k0:k0 + n, :] = pltpu.roll(e0, HEAD_DIM, 1).astype(BF16)
            bufs[1][0][k0:k0 + n, :] = pltpu.roll(o1, HEAD_DIM, 1).astype(BF16)
            bufs[1][1][k0:k0 + n, :] = o1.astype(BF16)

        uc = _dot(h, win_ref[:, 0:Q_OFF])
        cx = uc[:, CONV_CH:2 * CONV_CH] * uc[:, 2 * CONV_CH:3 * CONV_CH]
        c0 = CARRY_ROWS + r0
        cbuf[c0:c0 + n, :] = cx
        y = (w[2:3, :] * cx
             + w[1:2, :] * cbuf[c0 - 1:c0 - 1 + n, :]
             + w[0:1, :] * cbuf[c0 - 2:c0 - 2 + n, :])
        conv_n = _rms(uc[:, 0:CONV_CH] * y, gconv_ref[...]).astype(BF16)
        carried[r0] = (x, conv_n)

    def back(r0):
        n = SUB_TILE
        x, conv_n = carried.pop(r0)
        for jb in range(n // WINDOW):
            j = r0 // WINDOW + jb
            w0 = j * WINDOW
            for g in range(N_KV_HEADS):
                qg = qbuf[j, g]
                o = None
                for par in range(2):
                    s = lax.dot_general(qg, kbufs[g][par][w0:w0 + 2 * WINDOW, :],
                                        (((1,), (1,)), ((), ())),
                                        preferred_element_type=F32)
                    ps, inv_l = [], []
                    for sub in range(2):
                        sink = sinks_ref[layer, 4 * g + 2 * sub + par] * LOG2E
                        s_prev = s[sub * WINDOW:(sub + 1) * WINDOW, 0:WINDOW]
                        s_own = s[sub * WINDOW:(sub + 1) * WINDOW, WINDOW:2 * WINDOW]
                        if j == 0:
                            s_prev = jnp.where(no_prev, -jnp.inf, s_prev)
                        sh = jnp.where(own, s_own, s_prev)
                        m = jnp.maximum(jnp.max(sh, axis=-1, keepdims=True), sink)
                        p = jnp.exp2(sh - m)
                        l = jnp.sum(p, axis=-1, keepdims=True) + jnp.exp2(sink - m)
                        p_own = jnp.where(own, p, 0.0)
                        ps.append(jnp.concatenate([p - p_own, p_own], axis=1).astype(BF16))
                        inv_l.append(1.0 / l)
                    pv = _dot(jnp.concatenate(ps, axis=0), vbufs[g][par][w0:w0 + 2 * WINDOW, :])
                    o_par = [pv[sub * WINDOW:(sub + 1) * WINDOW] * inv_l[sub] for sub in range(2)]
                    o = o_par if o is None else [a + b for a, b in zip(o, o_par)]
                for sub in range(2):
                    c = 2 * g + sub
                    abuf[j * WINDOW:(j + 1) * WINDOW, c * LANES:(c + 1) * LANES] = o[sub]

        attn_n = _rms(abuf[r0:r0 + n, :], gattn_ref[...]).astype(BF16)
        o_ref[0, r0:r0 + n, :] = (x + _dot(conv_n, wout_ref[0:CONV_CH, :])
                                  + _dot(attn_n, wout_ref[CONV_CH:CONV_CH + ATTN_W, :]))

    starts = list(range(0, ts, SUB_TILE))
    front(starts[0])
    for i, r0 in enumerate(starts):
        if i + 1 < len(starts):
            front(starts[i + 1])
        back(r0)

    cbuf[0:CARRY_ROWS, :] = cbuf[ts:ts + CARRY_ROWS, :]
    for buf in kv_all:
        buf[0:WINDOW, :] = buf[ts:ts + WINDOW, :]


def _mix_layer(layer, x, cos, sin, sinks, gmix, w_in, conv_w, gconv, gattn, w_out):
    b, s, d = x.shape
    ts = SEQ_TILE
    tok = lambda w: pl.BlockSpec((1, ts, w), lambda i, j: (i, j, 0))
    per_layer = lambda *shape: _resident((None,) + shape, lambda i, j: (layer,) + (0,) * len(shape))
    kv_scratch = [pltpu.VMEM((ts + WINDOW, LANES), BF16)] * 8
    return pl.pallas_call(
        functools.partial(_mix_kernel, layer=layer),
        grid=(b, s // ts),
        in_specs=[pl.BlockSpec(memory_space=pltpu.SMEM),
                  tok(d), tok(LANES), tok(LANES),
                  per_layer(1, d), per_layer(d, IN_COLS), per_layer(CONV_K, CONV_CH),
                  per_layer(1, CONV_CH), per_layer(1, ATTN_W), per_layer(CONV_CH + ATTN_W, d)],
        out_specs=tok(d),
        out_shape=jax.ShapeDtypeStruct((b, s, d), F32),
        scratch_shapes=kv_scratch + [
            pltpu.VMEM((ts + CARRY_ROWS, CONV_CH), F32),
            pltpu.VMEM((ts // WINDOW, N_KV_HEADS, 2 * WINDOW, LANES), BF16),
            pltpu.VMEM((ts, ATTN_W), F32)],
        compiler_params=pltpu.CompilerParams(
            dimension_semantics=("arbitrary", "arbitrary"),
            vmem_limit_bytes=VMEM_LIMIT_BYTES),
        name="token_mix",
    )(sinks, x, cos, sin, gmix, w_in, conv_w, gconv, gattn, w_out)


def _memkv_kernel(mem_ref, g_ref, wkv_ref, kt_ref, v_ref):
    memn = _rms(mem_ref[0], g_ref[...]).astype(BF16)
    kv = _dot(memn, wkv_ref[...])
    kt_ref[...] = kv[:, 0:D_MODEL].T.astype(BF16)
    v_ref[...] = kv[:, D_MODEL:2 * D_MODEL].astype(BF16)


def _mem_kv(mem, norm_mem_g, wx_kv):
    b, m, d = mem.shape
    depth = wx_kv.shape[0]
    return pl.pallas_call(
        _memkv_kernel,
        grid=(depth, b),
        in_specs=[pl.BlockSpec((1, m, d), lambda l, i: (i, 0, 0)),
                  pl.BlockSpec((None, 1, d), lambda l, i: (l, 0, 0)),
                  pl.BlockSpec((None, d, 2 * d), lambda l, i: (l, 0, 0))],
        out_specs=[pl.BlockSpec((None, None, d, m), lambda l, i: (l, i, 0, 0)),
                   pl.BlockSpec((None, None, m, d), lambda l, i: (l, i, 0, 0))],
        out_shape=[jax.ShapeDtypeStruct((depth, b, d, m), BF16),
                   jax.ShapeDtypeStruct((depth, b, m, d), BF16)],
        compiler_params=pltpu.CompilerParams(vmem_limit_bytes=VMEM_LIMIT_BYTES),
        name="mem_kv",
    )(mem, norm_mem_g, wx_kv)


def _xattn_kernel(x_ref, g_ref, wq_ref, kt_ref, v_ref, wo_ref, o_ref):
    x = x_ref[0]
    h = _rms(x, g_ref[...]).astype(BF16)
    scale = 1.0 / math.sqrt(X_HEAD_DIM)
    q = (_dot(h, wq_ref[...]) * scale).astype(BF16)
    outs = []
    for hd in range(N_X_HEADS):
        sl = slice(hd * X_HEAD_DIM, (hd + 1) * X_HEAD_DIM)
        s = _dot(q[:, sl], kt_ref[sl, :])
        m = jnp.max(s, axis=-1, keepdims=True)
        p = jnp.exp(s - m)
        l = jnp.sum(p, axis=-1, keepdims=True)
        o = _dot(p.astype(BF16), v_ref[:, sl]) * (1.0 / l)
        outs.append(o.astype(BF16))
    o_ref[0] = x + _dot(jnp.concatenate(outs, axis=1), wo_ref[...])


def _xattn_layer(layer, x, g, wq, kt, v, wo):
    b, s, d = x.shape
    ts = SEQ_TILE
    tok = pl.BlockSpec((1, ts, d), lambda i, j: (i, j, 0))
    per_layer = lambda *shape: _resident((None,) + shape, lambda i, j: (layer,) + (0,) * len(shape))
    return pl.pallas_call(
        _xattn_kernel,
        grid=(b, s // ts),
        in_specs=[tok, per_layer(1, d), per_layer(d, d),
                  pl.BlockSpec((None, None, d, N_MEM), lambda i, j: (layer, i, 0, 0)),
                  pl.BlockSpec((None, None, N_MEM, d), lambda i, j: (layer, i, 0, 0)),
                  per_layer(d, d)],
        out_specs=tok,
        out_shape=jax.ShapeDtypeStruct((b, s, d), F32),
        compiler_params=pltpu.CompilerParams(vmem_limit_bytes=VMEM_LIMIT_BYTES),
        name="mem_xattn",
    )(x, g, wq, kt, v, wo)


def _mlp_kernel(x_ref, g_ref, wup_ref, wdn_ref, *rest, final):
    o_ref = rest[-1]
    x = x_ref[...]
    h = _rms(x, g_ref[...]).astype(BF16)
    acc = x
    for c in range(D_FF // FF_CHUNK):
        sl = slice(c * FF_CHUNK, (c + 1) * FF_CHUNK)
        up = _dot(h, wup_ref[:, sl])
        act = jnp.square(jnp.maximum(up, 0.0)).astype(BF16)
        acc = acc + _dot(act, wdn_ref[sl, :])
    if final:
        acc = _rms(acc, rest[0][...])
    o_ref[...] = acc


def _mlp_layer(layer, x2d, g, w_up, w_down, final_g):
    n, d = x2d.shape
    tm = MLP_TILE
    final = final_g is not None
    tok = pl.BlockSpec((tm, d), lambda i: (i, 0))
    per_layer = lambda *shape: _resident((None,) + shape, lambda i: (layer,) + (0,) * len(shape))
    in_specs = [tok, per_layer(1, d), per_layer(d, D_FF), per_layer(D_FF, d)]
    args = [x2d, g, w_up, w_down]
    if final:
        in_specs.append(_resident((1, d), lambda i: (0, 0)))
        args.append(final_g.reshape(1, d))
    return pl.pallas_call(
        functools.partial(_mlp_kernel, final=final),
        grid=(n // tm,),
        in_specs=in_specs,
        out_specs=tok,
        out_shape=jax.ShapeDtypeStruct((n, d), F32),
        compiler_params=pltpu.CompilerParams(vmem_limit_bytes=VMEM_LIMIT_BYTES),
        name="relu2_mlp",
    )(*args)


def kernel(x, mem, positions, norm_mix_g, w_in, conv_w, sinks, gnorm_conv_g, gnorm_attn_g, w_out, norm_x_g, norm_mem_g, wx_q, wx_kv, wx_o, norm_mlp_g, w_up, w_down, final_g):
    b, s, d = x.shape
    depth = w_in.shape[0]
    row = lambda g: g.reshape(depth, 1, g.shape[-1])
    cos, sin = _rope_tables(positions)
    kt, v = _mem_kv(mem, row(norm_mem_g), wx_kv)
    for l in range(depth):
        x = _mix_layer(l, x, cos, sin, sinks, row(norm_mix_g), w_in, conv_w,
                       row(gnorm_conv_g), row(gnorm_attn_g), w_out)
        x = _xattn_layer(l, x, row(norm_x_g), wx_q, kt, v, wx_o)
        x = _mlp_layer(l, x.reshape(b * s, d), row(norm_mlp_g), w_up, w_down,
                       final_g if l == depth - 1 else None).reshape(b, s, d)
    return x
```

```python
import functools
import math

import jax
import jax.numpy as jnp
from jax import lax
from jax.experimental import pallas as pl
from jax.experimental.pallas import tpu as pltpu

D_MODEL = 1024
N_MEM = 256
CONV_CH = 512
CONV_K = 3
HEAD_DIM = 64
N_Q_HEADS = 8
N_KV_HEADS = 2
ATTN_W = N_Q_HEADS * HEAD_DIM
KV_W = N_KV_HEADS * HEAD_DIM
WINDOW = 128
ROPE_THETA = 10000.0
N_X_HEADS = 4
X_HEAD_DIM = D_MODEL // N_X_HEADS
D_FF = 4 * D_MODEL
EPS = 1e-6
LOG2E = math.log2(math.e)
IN_COLS = 3 * CONV_CH + ATTN_W + 2 * KV_W
Q_OFF = 3 * CONV_CH
K_OFF = Q_OFF + ATTN_W

LANES = 128
SUBLANES = 8
VMEM_LIMIT_BYTES = 56 * 1024 * 1024

SEQ_TILE = 1024
SUB_TILE = 256
MLP_TILE = 512
FF_CHUNK = 1024
CARRY_ROWS = SUBLANES

BF16 = jnp.bfloat16
F32 = jnp.float32


def _rms(x, g):
    ms = jnp.mean(x * x, axis=-1, keepdims=True)
    return x * lax.rsqrt(ms + EPS) * g


def _dot(a, b):
    return jnp.dot(a, b.astype(BF16), preferred_element_type=F32)


def _resident(shape, index_map):
    return pl.BlockSpec(shape, index_map, pipeline_mode=pl.Buffered(1))


def _rope_kernel(pos_ref, inv_ref, cos_ref, sin_ref):
    ang = inv_ref[...] * pos_ref[0].astype(F32)
    c = jnp.cos(ang)
    s = jnp.sin(ang)
    cos_ref[0] = jnp.concatenate([c, c, c, c], axis=0).T
    sin_ref[0] = jnp.concatenate([-s, s, -s, s], axis=0).T


def _rope_tables(positions):
    b, s = positions.shape
    half = HEAD_DIM // 2
    inv_freq = ROPE_THETA ** (-jnp.arange(0, HEAD_DIM, 2, dtype=F32) / HEAD_DIM)
    table = pl.BlockSpec((1, s, LANES), lambda i: (i, 0, 0))
    return pl.pallas_call(
        _rope_kernel,
        grid=(b,),
        in_specs=[pl.BlockSpec((1, 1, s), lambda i: (i, 0, 0)),
                  pl.BlockSpec((half, 1), lambda i: (0, 0))],
        out_specs=[table, table],
        out_shape=[jax.ShapeDtypeStruct((b, s, LANES), F32)] * 2,
        name="rope_tables",
    )(positions.reshape(b, 1, s), inv_freq.reshape(half, 1))


def _mix_kernel(sinks_ref, x_ref, cos_ref, sin_ref, gmix_ref, win_ref, convw_ref,
                gconv_ref, gattn_ref, wout_ref, gx_ref, wq_ref, kt_ref, v_ref, wo_ref, o_ref,
                ke0, ko0, ke1, ko1, ve0, vo0, ve1, vo1, cbuf, qbuf, abuf, *, layer):
    ts = SEQ_TILE
    kbufs = ((ke0, ko0), (ke1, ko1))
    vbufs = ((ve0, vo0), (ve1, vo1))
    kv_all = (ke0, ko0, ke1, ko1, ve0, vo0, ve1, vo1)
    seq_start = pl.program_id(1) == 0

    @pl.when(seq_start)
    def _():
        for buf in kv_all:
            buf[0:WINDOW, :] = jnp.zeros((WINDOW, LANES), BF16)
        cbuf[0:CARRY_ROWS, :] = jnp.zeros((CARRY_ROWS, CONV_CH), F32)

    half_dim = HEAD_DIM // 2
    scale = LOG2E / math.sqrt(HEAD_DIM)
    w = convw_ref[...]

    qi = lax.broadcasted_iota(jnp.int32, (WINDOW, WINDOW), 0)
    ki = lax.broadcasted_iota(jnp.int32, (WINDOW, WINDOW), 1)
    own = ki <= qi
    no_prev = ki < WINDOW * seq_start.astype(jnp.int32)

    carried = {}

    def front(r0):
        n = SUB_TILE
        x = x_ref[0, r0:r0 + n, :]
        h = _rms(x, gmix_ref[...]).astype(BF16)

        cosf = cos_ref[0, r0:r0 + n, :]
        lane = lax.broadcasted_iota(jnp.int32, (n, LANES), 1)
        first_half = (lane & half_dim) == 0
        low_head = lane < HEAD_DIM
        sins = sin_ref[0, r0:r0 + n, :]

        def rope(t):
            rot = jnp.where(first_half,
                            pltpu.roll(t, LANES - half_dim, 1),
                            pltpu.roll(t, half_dim, 1))
            return t * cosf + rot * sins

        uq = _dot(h, win_ref[:, Q_OFF:K_OFF])
        for c in range(ATTN_W // LANES):
            qc = (rope(uq[:, c * LANES:(c + 1) * LANES]) * scale).astype(BF16)
            for jb in range(n // WINDOW):
                qbuf[r0 // WINDOW + jb, c // 2, (c % 2) * WINDOW:(c % 2 + 1) * WINDOW, :] = (
                    qc[jb * WINDOW:(jb + 1) * WINDOW])

        ukv = _dot(h, win_ref[:, K_OFF:IN_COLS])
        k = rope(ukv[:, 0:KV_W])
        v = ukv[:, KV_W:2 * KV_W]
        k0 = WINDOW + r0
        for src, bufs in ((k, kbufs), (v, vbufs)):
            e0 = jnp.where(low_head, src, 0.0)
            o1 = jnp.where(low_head, 0.0, src)
            bufs[0][0][k0:k0 + n, :] = e0.astype(BF16)
            bufs[0][1][k0:k0 + n, :] = pltpu.roll(e0, HEAD_DIM, 1).astype(BF16)
            bufs[1][0][k0:k0 + n, :] = pltpu.roll(o1, HEAD_DIM, 1).astype(BF16)
            bufs[1][1][k0:k0 + n, :] = o1.astype(BF16)

        uc = _dot(h, win_ref[:, 0:Q_OFF])
        cx = uc[:, CONV_CH:2 * CONV_CH] * uc[:, 2 * CONV_CH:3 * CONV_CH]
        c0 = CARRY_ROWS + r0
        cbuf[c0:c0 + n, :] = cx
        y = (w[2:3, :] * cx
             + w[1:2, :] * cbuf[c0 - 1:c0 - 1 + n, :]
             + w[0:1, :] * cbuf[c0 - 2:c0 - 2 + n, :])
        conv_n = _rms(uc[:, 0:CONV_CH] * y, gconv_ref[...]).astype(BF16)
        carried[r0] = (x, conv_n)

    def back(r0):
        n = SUB_TILE
        x, conv_n = carried.pop(r0)
        for jb in range(n // WINDOW):
            j = r0 // WINDOW + jb
            w0 = j * WINDOW
            for g in range(N_KV_HEADS):
                qg = qbuf[j, g]
                o = None
                for par in range(2):
                    s = lax.dot_general(qg, kbufs[g][par][w0:w0 + 2 * WINDOW, :],
                                        (((1,), (1,)), ((), ())),
                                        preferred_element_type=F32)
                    ps, inv_l = [], []
                    for sub in range(2):
                        sink = sinks_ref[layer, 4 * g + 2 * sub + par] * LOG2E
                        s_prev = s[sub * WINDOW:(sub + 1) * WINDOW, 0:WINDOW]
                        s_own = s[sub * WINDOW:(sub + 1) * WINDOW, WINDOW:2 * WINDOW]
                        if j == 0:
                            s_prev = jnp.where(no_prev, -jnp.inf, s_prev)
                        sh = jnp.where(own, s_own, s_prev)
                        m = jnp.maximum(jnp.max(sh, axis=-1, keepdims=True), sink)
                        p = jnp.exp2(sh - m)
                        l = jnp.sum(p, axis=-1, keepdims=True) + jnp.exp2(sink - m)
                        p_own = jnp.where(own, p, 0.0)
                        ps.append(jnp.concatenate([p - p_own, p_own], axis=1).astype(BF16))
                        inv_l.append(1.0 / l)
                    pv = _dot(jnp.concatenate(ps, axis=0), vbufs[g][par][w0:w0 + 2 * WINDOW, :])
                    o_par = [pv[sub * WINDOW:(sub + 1) * WINDOW] * inv_l[sub] for sub in range(2)]
                    o = o_par if o is None else [a + b for a, b in zip(o, o_par)]
                for sub in range(2):
                    c = 2 * g + sub
                    abuf[j * WINDOW:(j + 1) * WINDOW, c * LANES:(c + 1) * LANES] = o[sub]

        attn_n = _rms(abuf[r0:r0 + n, :], gattn_ref[...]).astype(BF16)
        return (x + _dot(conv_n, wout_ref[0:CONV_CH, :])
                + _dot(attn_n, wout_ref[CONV_CH:CONV_CH + ATTN_W, :]))

    def xattn(r0, x1):
        n = SUB_TILE
        h2 = _rms(x1, gx_ref[...]).astype(BF16)
        xscale = 1.0 / math.sqrt(X_HEAD_DIM)
        q = (_dot(h2, wq_ref[...]) * xscale).astype(BF16)
        outs = []
        for hd in range(N_X_HEADS):
            sl = slice(hd * X_HEAD_DIM, (hd + 1) * X_HEAD_DIM)
            s = _dot(q[:, sl], kt_ref[sl, :])
            m = jnp.max(s, axis=-1, keepdims=True)
            p = jnp.exp(s - m)
            l = jnp.sum(p, axis=-1, keepdims=True)
            outs.append((_dot(p.astype(BF16), v_ref[:, sl]) * (1.0 / l)).astype(BF16))
        o_ref[0, r0:r0 + n, :] = x1 + _dot(jnp.concatenate(outs, axis=1), wo_ref[...])

    starts = list(range(0, ts, SUB_TILE))
    front(starts[0])
    for i, r0 in enumerate(starts):
        if i + 1 < len(starts):
            front(starts[i + 1])
        x1 = back(r0)
        if i >= 1:
            xattn(*pending)
        pending = (r0, x1)
    xattn(*pending)

    cbuf[0:CARRY_ROWS, :] = cbuf[ts:ts + CARRY_ROWS, :]
    for buf in kv_all:
        buf[0:WINDOW, :] = buf[ts:ts + WINDOW, :]


def _mix_layer(layer, x, cos, sin, sinks, gmix, w_in, conv_w, gconv, gattn, w_out, gx, wq, kt, v, wo):
    b, s, d = x.shape
    ts = SEQ_TILE
    tok = lambda w: pl.BlockSpec((1, ts, w), lambda i, j: (i, j, 0))
    per_layer = lambda *shape: _resident((None,) + shape, lambda i, j: (layer,) + (0,) * len(shape))
    kv_scratch = [pltpu.VMEM((ts + WINDOW, LANES), BF16)] * 8
    return pl.pallas_call(
        functools.partial(_mix_kernel, layer=layer),
        grid=(b, s // ts),
        in_specs=[pl.BlockSpec(memory_space=pltpu.SMEM),
                  tok(d), tok(LANES), tok(LANES),
                  per_layer(1, d), per_layer(d, IN_COLS), per_layer(CONV_K, CONV_CH),
                  per_layer(1, CONV_CH), per_layer(1, ATTN_W), per_layer(CONV_CH + ATTN_W, d),
                  per_layer(1, d), per_layer(d, d),
                  pl.BlockSpec((None, None, d, N_MEM), lambda i, j: (layer, i, 0, 0)),
                  pl.BlockSpec((None, None, N_MEM, d), lambda i, j: (layer, i, 0, 0)),
                  per_layer(d, d)],
        out_specs=tok(d),
        out_shape=jax.ShapeDtypeStruct((b, s, d), F32),
        scratch_shapes=kv_scratch + [
            pltpu.VMEM((ts + CARRY_ROWS, CONV_CH), F32),
            pltpu.VMEM((ts // WINDOW, N_KV_HEADS, 2 * WINDOW, LANES), BF16),
            pltpu.VMEM((ts, ATTN_W), F32)],
        compiler_params=pltpu.CompilerParams(
            dimension_semantics=("arbitrary", "arbitrary"),
            vmem_limit_bytes=VMEM_LIMIT_BYTES),
        name="token_mix",
    )(sinks, x, cos, sin, gmix, w_in, conv_w, gconv, gattn, w_out, gx, wq, kt, v, wo)


def _memkv_kernel(mem_ref, g_ref, wkv_ref, kt_ref, v_ref):
    memn = _rms(mem_ref[0], g_ref[...]).astype(BF16)
    kv = _dot(memn, wkv_ref[...])
    kt_ref[...] = kv[:, 0:D_MODEL].T.astype(BF16)
    v_ref[...] = kv[:, D_MODEL:2 * D_MODEL].astype(BF16)


def _mem_kv(mem, norm_mem_g, wx_kv):
    b, m, d = mem.shape
    depth = wx_kv.shape[0]
    return pl.pallas_call(
        _memkv_kernel,
        grid=(depth, b),
        in_specs=[pl.BlockSpec((1, m, d), lambda l, i: (i, 0, 0)),
                  pl.BlockSpec((None, 1, d), lambda l, i: (l, 0, 0)),
                  pl.BlockSpec((None, d, 2 * d), lambda l, i: (l, 0, 0))],
        out_specs=[pl.BlockSpec((None, None, d, m), lambda l, i: (l, i, 0, 0)),
                   pl.BlockSpec((None, None, m, d), lambda l, i: (l, i, 0, 0))],
        out_shape=[jax.ShapeDtypeStruct((depth, b, d, m), BF16),
                   jax.ShapeDtypeStruct((depth, b, m, d), BF16)],
        compiler_params=pltpu.CompilerParams(vmem_limit_bytes=VMEM_LIMIT_BYTES),
        name="mem_kv",
    )(mem, norm_mem_g, wx_kv)


def _mlp_kernel(x_ref, g_ref, wup_ref, wdn_ref, *rest, final):
    o_ref = rest[-1]
    x = x_ref[...]
    h = _rms(x, g_ref[...]).astype(BF16)
    acc = x
    for c in range(D_FF // FF_CHUNK):
        sl = slice(c * FF_CHUNK, (c + 1) * FF_CHUNK)
        up = _dot(h, wup_ref[:, sl])
        act = jnp.square(jnp.maximum(up, 0.0)).astype(BF16)
        acc = acc + _dot(act, wdn_ref[sl, :])
    if final:
        acc = _rms(acc, rest[0][...])
    o_ref[...] = acc


def _mlp_layer(layer, x2d, g, w_up, w_down, final_g):
    n, d = x2d.shape
    tm = MLP_TILE
    final = final_g is not None
    tok = pl.BlockSpec((tm, d), lambda i: (i, 0))
    per_layer = lambda *shape: _resident((None,) + shape, lambda i: (layer,) + (0,) * len(shape))
    in_specs = [tok, per_layer(1, d), per_layer(d, D_FF), per_layer(D_FF, d)]
    args = [x2d, g, w_up, w_down]
    if final:
        in_specs.append(_resident((1, d), lambda i: (0, 0)))
        args.append(final_g.reshape(1, d))
    return pl.pallas_call(
        functools.partial(_mlp_kernel, final=final),
        grid=(n // tm,),
        in_specs=in_specs,
        out_specs=tok,
        out_shape=jax.ShapeDtypeStruct((n, d), F32),
        compiler_params=pltpu.CompilerParams(vmem_limit_bytes=VMEM_LIMIT_BYTES),
        name="relu2_mlp",
    )(*args)


def kernel(x, mem, positions, norm_mix_g, w_in, conv_w, sinks, gnorm_conv_g, gnorm_attn_g, w_out, norm_x_g, norm_mem_g, wx_q, wx_kv, wx_o, norm_mlp_g, w_up, w_down, final_g):
    b, s, d = x.shape
    depth = w_in.shape[0]
    row = lambda g: g.reshape(depth, 1, g.shape[-1])
    cos, sin = _rope_tables(positions)
    kt, v = _mem_kv(mem, row(norm_mem_g), wx_kv)
    for l in range(depth):
        x = _mix_layer(l, x, cos, sin, sinks, row(norm_mix_g), w_in, conv_w,
                       row(gnorm_conv_g), row(gnorm_attn_g), w_out,
                       row(norm_x_g), wx_q, kt, v, wx_o)
        x = _mlp_layer(l, x.reshape(b * s, d), row(norm_mlp_g), w_up, w_down,
                       final_g if l == depth - 1 else None).reshape(b, s, d)
    return x
```

```python
import functools
import math

import jax
import jax.numpy as jnp
from jax import lax
from jax.experimental import pallas as pl
from jax.experimental.pallas import tpu as pltpu

D_MODEL = 1024
N_MEM = 256
CONV_CH = 512
CONV_K = 3
HEAD_DIM = 64
N_Q_HEADS = 8
N_KV_HEADS = 2
ATTN_W = N_Q_HEADS * HEAD_DIM
KV_W = N_KV_HEADS * HEAD_DIM
WINDOW = 128
ROPE_THETA = 10000.0
N_X_HEADS = 4
X_HEAD_DIM = D_MODEL // N_X_HEADS
D_FF = 4 * D_MODEL
EPS = 1e-6
LOG2E = math.log2(math.e)
IN_COLS = 3 * CONV_CH + ATTN_W + 2 * KV_W
Q_OFF = 3 * CONV_CH
K_OFF = Q_OFF + ATTN_W

LANES = 128
SUBLANES = 8
VMEM_LIMIT_BYTES = 58 * 1024 * 1024

SEQ_TILE = 1024
SUB_TILE = 256
MLP_TILE = 1024
FF_CHUNK = 512
CARRY_ROWS = SUBLANES

BF16 = jnp.bfloat16
F32 = jnp.float32


def _rms(x, g):
    ms = jnp.mean(x * x, axis=-1, keepdims=True)
    return x * lax.rsqrt(ms + EPS) * g


def _dot(a, b):
    return jnp.dot(a, b.astype(BF16), preferred_element_type=F32)


def _resident(shape, index_map):
    return pl.BlockSpec(shape, index_map, pipeline_mode=pl.Buffered(1))


def _rope_kernel(pos_ref, inv_ref, cos_ref, sin_ref):
    ang = inv_ref[...] * pos_ref[0].astype(F32)
    c = jnp.cos(ang)
    s = jnp.sin(ang)
    cos_ref[0] = jnp.concatenate([c, c, c, c], axis=0).T
    sin_ref[0] = jnp.concatenate([-s, s, -s, s], axis=0).T


def _rope_tables(positions):
    b, s = positions.shape
    half = HEAD_DIM // 2
    inv_freq = ROPE_THETA ** (-jnp.arange(0, HEAD_DIM, 2, dtype=F32) / HEAD_DIM)
    table = pl.BlockSpec((1, s, LANES), lambda i: (i, 0, 0))
    return pl.pallas_call(
        _rope_kernel,
        grid=(b,),
        in_specs=[pl.BlockSpec((1, 1, s), lambda i: (i, 0, 0)),
                  pl.BlockSpec((half, 1), lambda i: (0, 0))],
        out_specs=[table, table],
        out_shape=[jax.ShapeDtypeStruct((b, s, LANES), F32)] * 2,
        name="rope_tables",
    )(positions.reshape(b, 1, s), inv_freq.reshape(half, 1))


def _mix_kernel(sinks_ref, x_ref, cos_ref, sin_ref, gmix_ref, win_ref, convw_ref,
                gconv_ref, gattn_ref, wout_ref, gx_ref, wq_ref, kt_ref, v_ref, wo_ref, o_ref,
                ke0, ko0, ke1, ko1, ve0, vo0, ve1, vo1, cbuf, qbuf, abuf, *, layer):
    ts = SEQ_TILE
    kbufs = ((ke0, ko0), (ke1, ko1))
    vbufs = ((ve0, vo0), (ve1, vo1))
    kv_all = (ke0, ko0, ke1, ko1, ve0, vo0, ve1, vo1)
    seq_start = pl.program_id(1) == 0

    @pl.when(seq_start)
    def _():
        for buf in kv_all:
            buf[0:WINDOW, :] = jnp.zeros((WINDOW, LANES), BF16)
        cbuf[0:CARRY_ROWS, :] = jnp.zeros((CARRY_ROWS, CONV_CH), F32)

    half_dim = HEAD_DIM // 2
    scale = LOG2E / math.sqrt(HEAD_DIM)
    w = convw_ref[...]

    qi = lax.broadcasted_iota(jnp.int32, (WINDOW, WINDOW), 0)
    ki = lax.broadcasted_iota(jnp.int32, (WINDOW, WINDOW), 1)
    own = ki <= qi
    no_prev = ki < WINDOW * seq_start.astype(jnp.int32)

    carried = {}

    def front(r0):
        n = SUB_TILE
        x = x_ref[0, r0:r0 + n, :]
        h = _rms(x, gmix_ref[...]).astype(BF16)

        cosf = cos_ref[0, r0:r0 + n, :]
        lane = lax.broadcasted_iota(jnp.int32, (n, LANES), 1)
        first_half = (lane & half_dim) == 0
        low_head = lane < HEAD_DIM
        sins = sin_ref[0, r0:r0 + n, :]

        def rope(t):
            rot = jnp.where(first_half,
                            pltpu.roll(t, LANES - half_dim, 1),
                            pltpu.roll(t, half_dim, 1))
            return t * cosf + rot * sins

        uq = _dot(h, win_ref[:, Q_OFF:K_OFF])
        for c in range(ATTN_W // LANES):
            qc = (rope(uq[:, c * LANES:(c + 1) * LANES]) * scale).astype(BF16)
            for jb in range(n // WINDOW):
                qbuf[r0 // WINDOW + jb, c // 2, (c % 2) * WINDOW:(c % 2 + 1) * WINDOW, :] = (
                    qc[jb * WINDOW:(jb + 1) * WINDOW])

        ukv = _dot(h, win_ref[:, K_OFF:IN_COLS])
        k = rope(ukv[:, 0:KV_W])
        v = ukv[:, KV_W:2 * KV_W]
        k0 = WINDOW + r0
        for src, bufs in ((k, kbufs), (v, vbufs)):
            e0 = jnp.where(low_head, src, 0.0)
            o1 = jnp.where(low_head, 0.0, src)
            bufs[0][0][k0:k0 + n, :] = e0.astype(BF16)
            bufs[0][1][k0:k0 + n, :] = pltpu.roll(e0, HEAD_DIM, 1).astype(BF16)
            bufs[1][0][k0:k0 + n, :] = pltpu.roll(o1, HEAD_DIM, 1).astype(BF16)
            bufs[1][1][k0:k0 + n, :] = o1.astype(BF16)

        uc = _dot(h, win_ref[:, 0:Q_OFF])
        cx = uc[:, CONV_CH:2 * CONV_CH] * uc[:, 2 * CONV_CH:3 * CONV_CH]
        c0 = CARRY_ROWS + r0
        cbuf[c0:c0 + n, :] = cx
        y = (w[2:3, :] * cx
             + w[1:2, :] * cbuf[c0 - 1:c0 - 1 + n, :]
             + w[0:1, :] * cbuf[c0 - 2:c0 - 2 + n, :])
        conv_n = _rms(uc[:, 0:CONV_CH] * y, gconv_ref[...]).astype(BF16)
        carried[r0] = (x, conv_n)

    def back(r0):
        n = SUB_TILE
        x, conv_n = carried.pop(r0)
        for jb in range(n // WINDOW):
            j = r0 // WINDOW + jb
            w0 = j * WINDOW
            for g in range(N_KV_HEADS):
                qg = qbuf[j, g]
                o = None
                for par in range(2):
                    s = lax.dot_general(qg, kbufs[g][par][w0:w0 + 2 * WINDOW, :],
                                        (((1,), (1,)), ((), ())),
                                        preferred_element_type=F32)
                    ps, inv_l = [], []
                    for sub in range(2):
                        sink = sinks_ref[layer, 4 * g + 2 * sub + par] * LOG2E
                        s_prev = s[sub * WINDOW:(sub + 1) * WINDOW, 0:WINDOW]
                        s_own = s[sub * WINDOW:(sub + 1) * WINDOW, WINDOW:2 * WINDOW]
                        if j == 0:
                            s_prev = jnp.where(no_prev, -jnp.inf, s_prev)
                        sh = jnp.where(own, s_own, s_prev)
                        m = jnp.maximum(jnp.max(sh, axis=-1, keepdims=True), sink)
                        p = jnp.exp2(sh - m)
                        l = jnp.sum(p, axis=-1, keepdims=True) + jnp.exp2(sink - m)
                        p_own = jnp.where(own, p, 0.0)
                        ps.append(jnp.concatenate([p - p_own, p_own], axis=1).astype(BF16))
                        inv_l.append(1.0 / l)
                    pv = _dot(jnp.concatenate(ps, axis=0), vbufs[g][par][w0:w0 + 2 * WINDOW, :])
                    o_par = [pv[sub * WINDOW:(sub + 1) * WINDOW] * inv_l[sub] for sub in range(2)]
                    o = o_par if o is None else [a + b for a, b in zip(o, o_par)]
                for sub in range(2):
                    c = 2 * g + sub
                    abuf[j * WINDOW:(j + 1) * WINDOW, c * LANES:(c + 1) * LANES] = o[sub]

        attn_n = _rms(abuf[r0:r0 + n, :], gattn_ref[...]).astype(BF16)
        return (x + _dot(conv_n, wout_ref[0:CONV_CH, :])
                + _dot(attn_n, wout_ref[CONV_CH:CONV_CH + ATTN_W, :]))

    def xattn(r0, x1):
        n = SUB_TILE
        h2 = _rms(x1, gx_ref[...]).astype(BF16)
        xscale = 1.0 / math.sqrt(X_HEAD_DIM)
        q = (_dot(h2, wq_ref[...]) * xscale).astype(BF16)
        outs = []
        for hd in range(N_X_HEADS):
            sl = slice(hd * X_HEAD_DIM, (hd + 1) * X_HEAD_DIM)
            s = _dot(q[:, sl], kt_ref[sl, :])
            m = jnp.max(s, axis=-1, keepdims=True)
            p = jnp.exp(s - m)
            l = jnp.sum(p, axis=-1, keepdims=True)
            outs.append((_dot(p.astype(BF16), v_ref[:, sl]) * (1.0 / l)).astype(BF16))
        o_ref[0, r0:r0 + n, :] = x1 + _dot(jnp.concatenate(outs, axis=1), wo_ref[...])

    starts = list(range(0, ts, SUB_TILE))
    front(starts[0])
    for i, r0 in enumerate(starts):
        if i + 1 < len(starts):
            front(starts[i + 1])
        x1 = back(r0)
        if i >= 1:
            xattn(*pending)
        pending = (r0, x1)
    xattn(*pending)

    cbuf[0:CARRY_ROWS, :] = cbuf[ts:ts + CARRY_ROWS, :]
    for buf in kv_all:
        buf[0:WINDOW, :] = buf[ts:ts + WINDOW, :]


def _mix_layer(layer, x, cos, sin, sinks, gmix, w_in, conv_w, gconv, gattn, w_out, gx, wq, kt, v, wo):
    b, s, d = x.shape
    ts = SEQ_TILE
    tok = lambda w: pl.BlockSpec((1, ts, w), lambda i, j: (i, j, 0))
    per_layer = lambda *shape: _resident((None,) + shape, lambda i, j: (layer,) + (0,) * len(shape))
    kv_scratch = [pltpu.VMEM((ts + WINDOW, LANES), BF16)] * 8
    return pl.pallas_call(
        functools.partial(_mix_kernel, layer=layer),
        grid=(b, s // ts),
        in_specs=[pl.BlockSpec(memory_space=pltpu.SMEM),
                  tok(d), tok(LANES), tok(LANES),
                  per_layer(1, d), per_layer(d, IN_COLS), per_layer(CONV_K, CONV_CH),
                  per_layer(1, CONV_CH), per_layer(1, ATTN_W), per_layer(CONV_CH + ATTN_W, d),
                  per_layer(1, d), per_layer(d, d),
                  pl.BlockSpec((None, None, d, N_MEM), lambda i, j: (layer, i, 0, 0)),
                  pl.BlockSpec((None, None, N_MEM, d), lambda i, j: (layer, i, 0, 0)),
                  per_layer(d, d)],
        out_specs=tok(d),
        out_shape=jax.ShapeDtypeStruct((b, s, d), F32),
        scratch_shapes=kv_scratch + [
            pltpu.VMEM((ts + CARRY_ROWS, CONV_CH), F32),
            pltpu.VMEM((ts // WINDOW, N_KV_HEADS, 2 * WINDOW, LANES), BF16),
            pltpu.VMEM((ts, ATTN_W), F32)],
        compiler_params=pltpu.CompilerParams(
            dimension_semantics=("arbitrary", "arbitrary"),
            vmem_limit_bytes=VMEM_LIMIT_BYTES),
        name="token_mix",
    )(sinks, x, cos, sin, gmix, w_in, conv_w, gconv, gattn, w_out, gx, wq, kt, v, wo)


def _memkv_kernel(mem_ref, g_ref, wkv_ref, kt_ref, v_ref):
    memn = _rms(mem_ref[0], g_ref[...]).astype(BF16)
    kv = _dot(memn, wkv_ref[...])
    kt_ref[...] = kv[:, 0:D_MODEL].T.astype(BF16)
    v_ref[...] = kv[:, D_MODEL:2 * D_MODEL].astype(BF16)


def _mem_kv(mem, norm_mem_g, wx_kv):
    b, m, d = mem.shape
    depth = wx_kv.shape[0]
    return pl.pallas_call(
        _memkv_kernel,
        grid=(depth, b),
        in_specs=[pl.BlockSpec((1, m, d), lambda l, i: (i, 0, 0)),
                  pl.BlockSpec((None, 1, d), lambda l, i: (l, 0, 0)),
                  pl.BlockSpec((None, d, 2 * d), lambda l, i: (l, 0, 0))],
        out_specs=[pl.BlockSpec((None, None, d, m), lambda l, i: (l, i, 0, 0)),
                   pl.BlockSpec((None, None, m, d), lambda l, i: (l, i, 0, 0))],
        out_shape=[jax.ShapeDtypeStruct((depth, b, d, m), BF16),
                   jax.ShapeDtypeStruct((depth, b, m, d), BF16)],
        compiler_params=pltpu.CompilerParams(vmem_limit_bytes=VMEM_LIMIT_BYTES),
        name="mem_kv",
    )(mem, norm_mem_g, wx_kv)


def _mlp_kernel(x_ref, g_ref, wup_ref, wdn_ref, *rest, final):
    o_ref = rest[-1]
    x = x_ref[...]
    h = _rms(x, g_ref[...]).astype(BF16)
    acc = x
    for c in range(D_FF // FF_CHUNK):
        sl = slice(c * FF_CHUNK, (c + 1) * FF_CHUNK)
        up = _dot(h, wup_ref[:, sl])
        act = jnp.square(jnp.maximum(up, 0.0)).astype(BF16)
        acc = acc + _dot(act, wdn_ref[sl, :])
    if final:
        acc = _rms(acc, rest[0][...])
    o_ref[...] = acc


def _mlp_layer(layer, x2d, g, w_up, w_down, final_g):
    n, d = x2d.shape
    tm = MLP_TILE
    final = final_g is not None
    tok = pl.BlockSpec((tm, d), lambda i: (i, 0))
    per_layer = lambda *shape: _resident((None,) + shape, lambda i: (layer,) + (0,) * len(shape))
    in_specs = [tok, per_layer(1, d), per_layer(d, D_FF), per_layer(D_FF, d)]
    args = [x2d, g, w_up, w_down]
    if final:
        in_specs.append(_resident((1, d), lambda i: (0, 0)))
        args.append(final_g.reshape(1, d))
    return pl.pallas_call(
        functools.partial(_mlp_kernel, final=final),
        grid=(n // tm,),
        in_specs=in_specs,
        out_specs=tok,
        out_shape=jax.ShapeDtypeStruct((n, d), F32),
        compiler_params=pltpu.CompilerParams(vmem_limit_bytes=VMEM_LIMIT_BYTES),
        name="relu2_mlp",
    )(*args)


def kernel(x, mem, positions, norm_mix_g, w_in, conv_w, sinks, gnorm_conv_g, gnorm_attn_g, w_out, norm_x_g, norm_mem_g, wx_q, wx_kv, wx_o, norm_mlp_g, w_up, w_down, final_g):
    b, s, d = x.shape
    depth = w_in.shape[0]
    row = lambda g: g.reshape(depth, 1, g.shape[-1])
    cos, sin = _rope_tables(positions)
    kt, v = _mem_kv(mem, row(norm_mem_g), wx_kv)
    for l in range(depth):
        x = _mix_layer(l, x, cos, sin, sinks, row(norm_mix_g), w_in, conv_w,
                       row(gnorm_conv_g), row(gnorm_attn_g), w_out,
                       row(norm_x_g), wx_q, kt, v, wx_o)
        x = _mlp_layer(l, x.reshape(b * s, d), row(norm_mlp_g), w_up, w_down,
                       final_g if l == depth - 1 else None).reshape(b, s, d)
    return x
```

```python
import functools
import math

import jax
import jax.numpy as jnp
from jax import lax
from jax.experimental import pallas as pl
from jax.experimental.pallas import tpu as pltpu

D_MODEL = 1024
N_MEM = 256
CONV_CH = 512
CONV_K = 3
HEAD_DIM = 64
N_Q_HEADS = 8
N_KV_HEADS = 2
ATTN_W = N_Q_HEADS * HEAD_DIM
KV_W = N_KV_HEADS * HEAD_DIM
WINDOW = 128
ROPE_THETA = 10000.0
N_X_HEADS = 4
X_HEAD_DIM = D_MODEL // N_X_HEADS
D_FF = 4 * D_MODEL
EPS = 1e-6
LOG2E = math.log2(math.e)
IN_COLS = 3 * CONV_CH + ATTN_W + 2 * KV_W
Q_OFF = 3 * CONV_CH
K_OFF = Q_OFF + ATTN_W

LANES = 128
SUBLANES = 8
VMEM_LIMIT_BYTES = 58 * 1024 * 1024

SEQ_TILE = 1024
SUB_TILE = 256
FRONT_AHEAD = 3
MLP_TILE = 1024
FF_CHUNK = 512
CARRY_ROWS = SUBLANES

BF16 = jnp.bfloat16
F32 = jnp.float32


def _rms(x, g):
    ms = jnp.mean(x * x, axis=-1, keepdims=True)
    return x * lax.rsqrt(ms + EPS) * g


def _dot(a, b):
    return jnp.dot(a, b.astype(BF16), preferred_element_type=F32)


def _resident(shape, index_map):
    return pl.BlockSpec(shape, index_map, pipeline_mode=pl.Buffered(1))


def _rope_kernel(pos_ref, inv_ref, cos_ref, sin_ref):
    ang = inv_ref[...] * pos_ref[0].astype(F32)
    c = jnp.cos(ang)
    s = jnp.sin(ang)
    cos_ref[0] = jnp.concatenate([c, c, c, c], axis=0).T
    sin_ref[0] = jnp.concatenate([-s, s, -s, s], axis=0).T


def _rope_tables(positions):
    b, s = positions.shape
    half = HEAD_DIM // 2
    inv_freq = ROPE_THETA ** (-jnp.arange(0, HEAD_DIM, 2, dtype=F32) / HEAD_DIM)
    table = pl.BlockSpec((1, s, LANES), lambda i: (i, 0, 0))
    return pl.pallas_call(
        _rope_kernel,
        grid=(b,),
        in_specs=[pl.BlockSpec((1, 1, s), lambda i: (i, 0, 0)),
                  pl.BlockSpec((half, 1), lambda i: (0, 0))],
        out_specs=[table, table],
        out_shape=[jax.ShapeDtypeStruct((b, s, LANES), F32)] * 2,
        name="rope_tables",
    )(positions.reshape(b, 1, s), inv_freq.reshape(half, 1))


def _mix_kernel(sinks_ref, x_ref, cos_ref, sin_ref, gmix_ref, win_ref, convw_ref,
                gconv_ref, gattn_ref, wout_ref, gx_ref, wq_ref, kt_ref, v_ref, wo_ref, o_ref,
                ke0, ko0, ke1, ko1, ve0, vo0, ve1, vo1, cbuf, qbuf, abuf, *, layer):
    ts = SEQ_TILE
    kbufs = ((ke0, ko0), (ke1, ko1))
    vbufs = ((ve0, vo0), (ve1, vo1))
    kv_all = (ke0, ko0, ke1, ko1, ve0, vo0, ve1, vo1)
    seq_start = pl.program_id(1) == 0

    @pl.when(seq_start)
    def _():
        for buf in kv_all:
            buf[0:WINDOW, :] = jnp.zeros((WINDOW, LANES), BF16)
        cbuf[0:CARRY_ROWS, :] = jnp.zeros((CARRY_ROWS, CONV_CH), F32)

    half_dim = HEAD_DIM // 2
    scale = LOG2E / math.sqrt(HEAD_DIM)
    w = convw_ref[...]

    qi = lax.broadcasted_iota(jnp.int32, (WINDOW, WINDOW), 0)
    ki = lax.broadcasted_iota(jnp.int32, (WINDOW, WINDOW), 1)
    own = ki <= qi
    no_prev = ki < WINDOW * seq_start.astype(jnp.int32)

    carried = {}

    def front(r0):
        n = SUB_TILE
        x = x_ref[0, r0:r0 + n, :]
        h = _rms(x, gmix_ref[...]).astype(BF16)

        cosf = cos_ref[0, r0:r0 + n, :]
        lane = lax.broadcasted_iota(jnp.int32, (n, LANES), 1)
        first_half = (lane & half_dim) == 0
        low_head = lane < HEAD_DIM
        sins = sin_ref[0, r0:r0 + n, :]

        def rope(t):
            rot = jnp.where(first_half,
                            pltpu.roll(t, LANES - half_dim, 1),
                            pltpu.roll(t, half_dim, 1))
            return t * cosf + rot * sins

        uq = _dot(h, win_ref[:, Q_OFF:K_OFF])
        for c in range(ATTN_W // LANES):
            qc = (rope(uq[:, c * LANES:(c + 1) * LANES]) * scale).astype(BF16)
            for jb in range(n // WINDOW):
                qbuf[r0 // WINDOW + jb, c // 2, (c % 2) * WINDOW:(c % 2 + 1) * WINDOW, :] = (
                    qc[jb * WINDOW:(jb + 1) * WINDOW])

        ukv = _dot(h, win_ref[:, K_OFF:IN_COLS])
        k = rope(ukv[:, 0:KV_W])
        v = ukv[:, KV_W:2 * KV_W]
        k0 = WINDOW + r0
        for src, bufs in ((k, kbufs), (v, vbufs)):
            e0 = jnp.where(low_head, src, 0.0)
            o1 = jnp.where(low_head, 0.0, src)
            bufs[0][0][k0:k0 + n, :] = e0.astype(BF16)
            bufs[0][1][k0:k0 + n, :] = pltpu.roll(e0, HEAD_DIM, 1).astype(BF16)
            bufs[1][0][k0:k0 + n, :] = pltpu.roll(o1, HEAD_DIM, 1).astype(BF16)
            bufs[1][1][k0:k0 + n, :] = o1.astype(BF16)

        uc = _dot(h, win_ref[:, 0:Q_OFF])
        cx = uc[:, CONV_CH:2 * CONV_CH] * uc[:, 2 * CONV_CH:3 * CONV_CH]
        c0 = CARRY_ROWS + r0
        cbuf[c0:c0 + n, :] = cx
        y = (w[2:3, :] * cx
             + w[1:2, :] * cbuf[c0 - 1:c0 - 1 + n, :]
             + w[0:1, :] * cbuf[c0 - 2:c0 - 2 + n, :])
        conv_n = _rms(uc[:, 0:CONV_CH] * y, gconv_ref[...]).astype(BF16)
        carried[r0] = (x, conv_n)

    def back(r0):
        n = SUB_TILE
        x, conv_n = carried.pop(r0)
        for jb in range(n // WINDOW):
            j = r0 // WINDOW + jb
            w0 = j * WINDOW
            for g in range(N_KV_HEADS):
                qg = qbuf[j, g]
                o = None
                for par in range(2):
                    s = lax.dot_general(qg, kbufs[g][par][w0:w0 + 2 * WINDOW, :],
                                        (((1,), (1,)), ((), ())),
                                        preferred_element_type=F32)
                    ps, inv_l = [], []
                    for sub in range(2):
                        sink = sinks_ref[layer, 4 * g + 2 * sub + par] * LOG2E
                        s_prev = s[sub * WINDOW:(sub + 1) * WINDOW, 0:WINDOW]
                        s_own = s[sub * WINDOW:(sub + 1) * WINDOW, WINDOW:2 * WINDOW]
                        if j == 0:
                            s_prev = jnp.where(no_prev, -jnp.inf, s_prev)
                        sh = jnp.where(own, s_own, s_prev)
                        m = jnp.maximum(jnp.max(sh, axis=-1, keepdims=True), sink)
                        p = jnp.exp2(sh - m)
                        l = jnp.sum(p, axis=-1, keepdims=True) + jnp.exp2(sink - m)
                        p_own = jnp.where(own, p, 0.0)
                        ps.append(jnp.concatenate([p - p_own, p_own], axis=1).astype(BF16))
                        inv_l.append(1.0 / l)
                    pv = _dot(jnp.concatenate(ps, axis=0), vbufs[g][par][w0:w0 + 2 * WINDOW, :])
                    o_par = [pv[sub * WINDOW:(sub + 1) * WINDOW] * inv_l[sub] for sub in range(2)]
                    o = o_par if o is None else [a + b for a, b in zip(o, o_par)]
                for sub in range(2):
                    c = 2 * g + sub
                    abuf[j * WINDOW:(j + 1) * WINDOW, c * LANES:(c + 1) * LANES] = o[sub]

        attn_n = _rms(abuf[r0:r0 + n, :], gattn_ref[...]).astype(BF16)
        return (x + _dot(conv_n, wout_ref[0:CONV_CH, :])
                + _dot(attn_n, wout_ref[CONV_CH:CONV_CH + ATTN_W, :]))

    def xattn(r0, x1):
        n = SUB_TILE
        h2 = _rms(x1, gx_ref[...]).astype(BF16)
        xscale = 1.0 / math.sqrt(X_HEAD_DIM)
        q = (_dot(h2, wq_ref[...]) * xscale).astype(BF16)
        outs = []
        for hd in range(N_X_HEADS):
            sl = slice(hd * X_HEAD_DIM, (hd + 1) * X_HEAD_DIM)
            s = _dot(q[:, sl], kt_ref[sl, :])
            m = jnp.max(s, axis=-1, keepdims=True)
            p = jnp.exp(s - m)
            l = jnp.sum(p, axis=-1, keepdims=True)
            outs.append((_dot(p.astype(BF16), v_ref[:, sl]) * (1.0 / l)).astype(BF16))
        o_ref[0, r0:r0 + n, :] = x1 + _dot(jnp.concatenate(outs, axis=1), wo_ref[...])

    starts = list(range(0, ts, SUB_TILE))
    for r0 in starts[:FRONT_AHEAD]:
        front(r0)
    x1 = {}
    for i, r0 in enumerate(starts):
        if i + FRONT_AHEAD < len(starts):
            front(starts[i + FRONT_AHEAD])
        x1[i] = back(r0)
        if i >= 1:
            xattn(starts[i - 1], x1.pop(i - 1))
    xattn(starts[-1], x1.pop(len(starts) - 1))

    cbuf[0:CARRY_ROWS, :] = cbuf[ts:ts + CARRY_ROWS, :]
    for buf in kv_all:
        buf[0:WINDOW, :] = buf[ts:ts + WINDOW, :]


def _mix_layer(layer, x, cos, sin, sinks, gmix, w_in, conv_w, gconv, gattn, w_out, gx, wq, kt, v, wo):
    b, s, d = x.shape
    ts = SEQ_TILE
    tok = lambda w: pl.BlockSpec((1, ts, w), lambda i, j: (i, j, 0))
    per_layer = lambda *shape: _resident((None,) + shape, lambda i, j: (layer,) + (0,) * len(shape))
    kv_scratch = [pltpu.VMEM((ts + WINDOW, LANES), BF16)] * 8
    return pl.pallas_call(
        functools.partial(_mix_kernel, layer=layer),
        grid=(b, s // ts),
        in_specs=[pl.BlockSpec(memory_space=pltpu.SMEM),
                  tok(d), tok(LANES), tok(LANES),
                  per_layer(1, d), per_layer(d, IN_COLS), per_layer(CONV_K, CONV_CH),
                  per_layer(1, CONV_CH), per_layer(1, ATTN_W), per_layer(CONV_CH + ATTN_W, d),
                  per_layer(1, d), per_layer(d, d),
                  pl.BlockSpec((None, None, d, N_MEM), lambda i, j: (layer, i, 0, 0)),
                  pl.BlockSpec((None, None, N_MEM, d), lambda i, j: (layer, i, 0, 0)),
                  per_layer(d, d)],
        out_specs=tok(d),
        out_shape=jax.ShapeDtypeStruct((b, s, d), F32),
        scratch_shapes=kv_scratch + [
            pltpu.VMEM((ts + CARRY_ROWS, CONV_CH), F32),
            pltpu.VMEM((ts // WINDOW, N_KV_HEADS, 2 * WINDOW, LANES), BF16),
            pltpu.VMEM((ts, ATTN_W), F32)],
        compiler_params=pltpu.CompilerParams(
            dimension_semantics=("arbitrary", "arbitrary"),
            vmem_limit_bytes=VMEM_LIMIT_BYTES),
        name="token_mix",
    )(sinks, x, cos, sin, gmix, w_in, conv_w, gconv, gattn, w_out, gx, wq, kt, v, wo)


def _memkv_kernel(mem_ref, g_ref, wkv_ref, kt_ref, v_ref):
    nb, m, d = mem_ref.shape
    memn = _rms(mem_ref[...].reshape(nb * m, d), g_ref[...]).astype(BF16)
    kv = _dot(memn, wkv_ref[...])
    for b in range(nb):
        rows = slice(b * m, (b + 1) * m)
        kt_ref[b] = kv[rows, 0:d].T.astype(BF16)
        v_ref[b] = kv[rows, d:2 * d].astype(BF16)


def _mem_kv(mem, norm_mem_g, wx_kv):
    b, m, d = mem.shape
    depth = wx_kv.shape[0]
    return pl.pallas_call(
        _memkv_kernel,
        grid=(depth,),
        in_specs=[pl.BlockSpec((b, m, d), lambda l: (0, 0, 0)),
                  pl.BlockSpec((None, 1, d), lambda l: (l, 0, 0)),
                  pl.BlockSpec((None, d, 2 * d), lambda l: (l, 0, 0))],
        out_specs=[pl.BlockSpec((None, b, d, m), lambda l: (l, 0, 0, 0)),
                   pl.BlockSpec((None, b, m, d), lambda l: (l, 0, 0, 0))],
        out_shape=[jax.ShapeDtypeStruct((depth, b, d, m), BF16),
                   jax.ShapeDtypeStruct((depth, b, m, d), BF16)],
        compiler_params=pltpu.CompilerParams(vmem_limit_bytes=VMEM_LIMIT_BYTES),
        name="mem_kv",
    )(mem, norm_mem_g, wx_kv)


def _mlp_kernel(x_ref, g_ref, wup_ref, wdn_ref, *rest, final):
    o_ref = rest[-1]
    x = x_ref[...]
    h = _rms(x, g_ref[...]).astype(BF16)
    acc = x
    for c in range(D_FF // FF_CHUNK):
        sl = slice(c * FF_CHUNK, (c + 1) * FF_CHUNK)
        up = _dot(h, wup_ref[:, sl])
        act = jnp.square(jnp.maximum(up, 0.0)).astype(BF16)
        acc = acc + _dot(act, wdn_ref[sl, :])
    if final:
        acc = _rms(acc, rest[0][...])
    o_ref[...] = acc


def _mlp_layer(layer, x2d, g, w_up, w_down, final_g):
    n, d = x2d.shape
    tm = MLP_TILE
    final = final_g is not None
    tok = pl.BlockSpec((tm, d), lambda i: (i, 0))
    per_layer = lambda *shape: _resident((None,) + shape, lambda i: (layer,) + (0,) * len(shape))
    in_specs = [tok, per_layer(1, d), per_layer(d, D_FF), per_layer(D_FF, d)]
    args = [x2d, g, w_up, w_down]
    if final:
        in_specs.append(_resident((1, d), lambda i: (0, 0)))
        args.append(final_g.reshape(1, d))
    return pl.pallas_call(
        functools.partial(_mlp_kernel, final=final),
        grid=(n // tm,),
        in_specs=in_specs,
        out_specs=tok,
        out_shape=jax.ShapeDtypeStruct((n, d), F32),
        compiler_params=pltpu.CompilerParams(vmem_limit_bytes=VMEM_LIMIT_BYTES),
        name="relu2_mlp",
    )(*args)


def kernel(x, mem, positions, norm_mix_g, w_in, conv_w, sinks, gnorm_conv_g, gnorm_attn_g, w_out, norm_x_g, norm_mem_g, wx_q, wx_kv, wx_o, norm_mlp_g, w_up, w_down, final_g):
    b, s, d = x.shape
    depth = w_in.shape[0]
    row = lambda g: g.reshape(depth, 1, g.shape[-1])
    cos, sin = _rope_tables(positions)
    kt, v = _mem_kv(mem, row(norm_mem_g), wx_kv)
    for l in range(depth):
        x = _mix_layer(l, x, cos, sin, sinks, row(norm_mix_g), w_in, conv_w,
                       row(gnorm_conv_g), row(gnorm_attn_g), w_out,
                       row(norm_x_g), wx_q, kt, v, wx_o)
        x = _mlp_layer(l, x.reshape(b * s, d), row(norm_mlp_g), w_up, w_down,
                       final_g if l == depth - 1 else None).reshape(b, s, d)
    return x
```

```python
import functools
import math

import jax
import jax.numpy as jnp
from jax import lax
from jax.experimental import pallas as pl
from jax.experimental.pallas import tpu as pltpu

D_MODEL = 1024
N_MEM = 256
CONV_CH = 512
CONV_K = 3
HEAD_DIM = 64
N_Q_HEADS = 8
N_KV_HEADS = 2
ATTN_W = N_Q_HEADS * HEAD_DIM
KV_W = N_KV_HEADS * HEAD_DIM
WINDOW = 128
ROPE_THETA = 10000.0
N_X_HEADS = 4
X_HEAD_DIM = D_MODEL // N_X_HEADS
D_FF = 4 * D_MODEL
EPS = 1e-6
LOG2E = math.log2(math.e)
IN_COLS = 3 * CONV_CH + ATTN_W + 2 * KV_W
Q_OFF = 3 * CONV_CH
K_OFF = Q_OFF + ATTN_W

LANES = 128
SUBLANES = 8
VMEM_LIMIT_BYTES = 58 * 1024 * 1024

SEQ_TILE = 1024
SUB_TILE = 256
FRONT_AHEAD = 3
MLP_TILE = 1024
FF_CHUNK = 512
CARRY_ROWS = SUBLANES

BF16 = jnp.bfloat16
F32 = jnp.float32


def _rms(x, g):
    ms = jnp.mean(x * x, axis=-1, keepdims=True)
    return x * lax.rsqrt(ms + EPS) * g


def _dot(a, b):
    return jnp.dot(a, b.astype(BF16), preferred_element_type=F32)


def _resident(shape, index_map):
    return pl.BlockSpec(shape, index_map, pipeline_mode=pl.Buffered(1))


def _rope_kernel(pos_ref, inv_ref, cos_ref, sin_ref):
    pos = pos_ref[pl.ds(pl.program_id(0), 1), :]
    ang = inv_ref[...] * pos.astype(F32)
    c = jnp.cos(ang)
    s = jnp.sin(ang)
    cos_ref[0] = jnp.concatenate([c, c, c, c], axis=0).T
    sin_ref[0] = jnp.concatenate([-s, s, -s, s], axis=0).T


def _rope_tables(positions):
    b, s = positions.shape
    half = HEAD_DIM // 2
    inv_freq = ROPE_THETA ** (-jnp.arange(0, HEAD_DIM, 2, dtype=F32) / HEAD_DIM)
    table = pl.BlockSpec((1, s, LANES), lambda i: (i, 0, 0))
    return pl.pallas_call(
        _rope_kernel,
        grid=(b,),
        in_specs=[pl.BlockSpec((b, s), lambda i: (0, 0)),
                  pl.BlockSpec((half, 1), lambda i: (0, 0))],
        out_specs=[table, table],
        out_shape=[jax.ShapeDtypeStruct((b, s, LANES), F32)] * 2,
        name="rope_tables",
    )(positions, inv_freq.reshape(half, 1))


def _mix_kernel(sinks_ref, x_ref, cos_ref, sin_ref, gmix_ref, win_ref, convw_ref,
                gconv_ref, gattn_ref, wout_ref, gx_ref, wq_ref, kt_ref, v_ref, wo_ref, o_ref,
                ke0, ko0, ke1, ko1, ve0, vo0, ve1, vo1, cbuf, qbuf, abuf, *, layer):
    ts = SEQ_TILE
    kbufs = ((ke0, ko0), (ke1, ko1))
    vbufs = ((ve0, vo0), (ve1, vo1))
    kv_all = (ke0, ko0, ke1, ko1, ve0, vo0, ve1, vo1)
    seq_start = pl.program_id(1) == 0

    @pl.when(seq_start)
    def _():
        for buf in kv_all:
            buf[0:WINDOW, :] = jnp.zeros((WINDOW, LANES), BF16)
        cbuf[0:CARRY_ROWS, :] = jnp.zeros((CARRY_ROWS, CONV_CH), F32)

    half_dim = HEAD_DIM // 2
    scale = LOG2E / math.sqrt(HEAD_DIM)
    w = convw_ref[...]

    qi = lax.broadcasted_iota(jnp.int32, (WINDOW, WINDOW), 0)
    ki = lax.broadcasted_iota(jnp.int32, (WINDOW, WINDOW), 1)
    own = ki <= qi
    no_prev = ki < WINDOW * seq_start.astype(jnp.int32)

    carried = {}

    def front(r0):
        n = SUB_TILE
        x = x_ref[0, r0:r0 + n, :]
        h = _rms(x, gmix_ref[layer:layer + 1, :]).astype(BF16)

        cosf = cos_ref[0, r0:r0 + n, :]
        lane = lax.broadcasted_iota(jnp.int32, (n, LANES), 1)
        first_half = (lane & half_dim) == 0
        low_head = lane < HEAD_DIM
        sins = sin_ref[0, r0:r0 + n, :]

        def rope(t):
            rot = jnp.where(first_half,
                            pltpu.roll(t, LANES - half_dim, 1),
                            pltpu.roll(t, half_dim, 1))
            return t * cosf + rot * sins

        uq = _dot(h, win_ref[:, Q_OFF:K_OFF])
        for c in range(ATTN_W // LANES):
            qc = (rope(uq[:, c * LANES:(c + 1) * LANES]) * scale).astype(BF16)
            for jb in range(n // WINDOW):
                qbuf[r0 // WINDOW + jb, c // 2, (c % 2) * WINDOW:(c % 2 + 1) * WINDOW, :] = (
                    qc[jb * WINDOW:(jb + 1) * WINDOW])

        ukv = _dot(h, win_ref[:, K_OFF:IN_COLS])
        k = rope(ukv[:, 0:KV_W])
        v = ukv[:, KV_W:2 * KV_W]
        k0 = WINDOW + r0
        for src, bufs in ((k, kbufs), (v, vbufs)):
            e0 = jnp.where(low_head, src, 0.0)
            o1 = jnp.where(low_head, 0.0, src)
            bufs[0][0][k0:k0 + n, :] = e0.astype(BF16)
            bufs[0][1][k0:k0 + n, :] = pltpu.roll(e0, HEAD_DIM, 1).astype(BF16)
            bufs[1][0][k0:k0 + n, :] = pltpu.roll(o1, HEAD_DIM, 1).astype(BF16)
            bufs[1][1][k0:k0 + n, :] = o1.astype(BF16)

        uc = _dot(h, win_ref[:, 0:Q_OFF])
        cx = uc[:, CONV_CH:2 * CONV_CH] * uc[:, 2 * CONV_CH:3 * CONV_CH]
        c0 = CARRY_ROWS + r0
        cbuf[c0:c0 + n, :] = cx
        y = (w[2:3, :] * cx
             + w[1:2, :] * cbuf[c0 - 1:c0 - 1 + n, :]
             + w[0:1, :] * cbuf[c0 - 2:c0 - 2 + n, :])
        conv_n = _rms(uc[:, 0:CONV_CH] * y, gconv_ref[layer:layer + 1, :]).astype(BF16)
        carried[r0] = (x, conv_n)

    def back(r0):
        n = SUB_TILE
        x, conv_n = carried.pop(r0)
        for jb in range(n // WINDOW):
            j = r0 // WINDOW + jb
            w0 = j * WINDOW
            for g in range(N_KV_HEADS):
                qg = qbuf[j, g]
                o = None
                for par in range(2):
                    s = lax.dot_general(qg, kbufs[g][par][w0:w0 + 2 * WINDOW, :],
                                        (((1,), (1,)), ((), ())),
                                        preferred_element_type=F32)
                    ps, inv_l = [], []
                    for sub in range(2):
                        sink = sinks_ref[layer, 4 * g + 2 * sub + par] * LOG2E
                        s_prev = s[sub * WINDOW:(sub + 1) * WINDOW, 0:WINDOW]
                        s_own = s[sub * WINDOW:(sub + 1) * WINDOW, WINDOW:2 * WINDOW]
                        if j == 0:
                            s_prev = jnp.where(no_prev, -jnp.inf, s_prev)
                        sh = jnp.where(own, s_own, s_prev)
                        m = jnp.maximum(jnp.max(sh, axis=-1, keepdims=True), sink)
                        p = jnp.exp2(sh - m)
                        l = jnp.sum(p, axis=-1, keepdims=True) + jnp.exp2(sink - m)
                        p_own = jnp.where(own, p, 0.0)
                        ps.append(jnp.concatenate([p - p_own, p_own], axis=1).astype(BF16))
                        inv_l.append(1.0 / l)
                    pv = _dot(jnp.concatenate(ps, axis=0), vbufs[g][par][w0:w0 + 2 * WINDOW, :])
                    o_par = [pv[sub * WINDOW:(sub + 1) * WINDOW] * inv_l[sub] for sub in range(2)]
                    o = o_par if o is None else [a + b for a, b in zip(o, o_par)]
                for sub in range(2):
                    c = 2 * g + sub
                    abuf[j * WINDOW:(j + 1) * WINDOW, c * LANES:(c + 1) * LANES] = o[sub]

        attn_n = _rms(abuf[r0:r0 + n, :], gattn_ref[layer:layer + 1, :]).astype(BF16)
        return (x + _dot(conv_n, wout_ref[0:CONV_CH, :])
                + _dot(attn_n, wout_ref[CONV_CH:CONV_CH + ATTN_W, :]))

    def xattn(r0, x1):
        n = SUB_TILE
        h2 = _rms(x1, gx_ref[layer:layer + 1, :]).astype(BF16)
        xscale = 1.0 / math.sqrt(X_HEAD_DIM)
        q = (_dot(h2, wq_ref[...]) * xscale).astype(BF16)
        outs = []
        for hd in range(N_X_HEADS):
            sl = slice(hd * X_HEAD_DIM, (hd + 1) * X_HEAD_DIM)
            s = _dot(q[:, sl], kt_ref[sl, :])
            m = jnp.max(s, axis=-1, keepdims=True)
            p = jnp.exp(s - m)
            l = jnp.sum(p, axis=-1, keepdims=True)
            outs.append((_dot(p.astype(BF16), v_ref[:, sl]) * (1.0 / l)).astype(BF16))
        o_ref[0, r0:r0 + n, :] = x1 + _dot(jnp.concatenate(outs, axis=1), wo_ref[...])

    starts = list(range(0, ts, SUB_TILE))
    for r0 in starts[:FRONT_AHEAD]:
        front(r0)
    x1 = {}
    for i, r0 in enumerate(starts):
        if i + FRONT_AHEAD < len(starts):
            front(starts[i + FRONT_AHEAD])
        x1[i] = back(r0)
        if i >= 1:
            xattn(starts[i - 1], x1.pop(i - 1))
    xattn(starts[-1], x1.pop(len(starts) - 1))

    cbuf[0:CARRY_ROWS, :] = cbuf[ts:ts + CARRY_ROWS, :]
    for buf in kv_all:
        buf[0:WINDOW, :] = buf[ts:ts + WINDOW, :]


def _mix_layer(layer, x, cos, sin, sinks, gmix, w_in, conv_w, gconv, gattn, w_out, gx, wq, kt, v, wo):
    b, s, d = x.shape
    ts = SEQ_TILE
    tok = lambda w: pl.BlockSpec((1, ts, w), lambda i, j: (i, j, 0))
    per_layer = lambda *shape: _resident((None,) + shape, lambda i, j: (layer,) + (0,) * len(shape))
    gains = lambda w: _resident((gmix.shape[0], w), lambda i, j: (0, 0))
    kv_scratch = [pltpu.VMEM((ts + WINDOW, LANES), BF16)] * 8
    return pl.pallas_call(
        functools.partial(_mix_kernel, layer=layer),
        grid=(b, s // ts),
        in_specs=[pl.BlockSpec(memory_space=pltpu.SMEM),
                  tok(d), tok(LANES), tok(LANES),
                  gains(d), per_layer(d, IN_COLS), per_layer(CONV_K, CONV_CH),
                  gains(CONV_CH), gains(ATTN_W), per_layer(CONV_CH + ATTN_W, d),
                  gains(d), per_layer(d, d),
                  pl.BlockSpec((None, None, d, N_MEM), lambda i, j: (layer, i, 0, 0)),
                  pl.BlockSpec((None, None, N_MEM, d), lambda i, j: (layer, i, 0, 0)),
                  per_layer(d, d)],
        out_specs=tok(d),
        out_shape=jax.ShapeDtypeStruct((b, s, d), F32),
        scratch_shapes=kv_scratch + [
            pltpu.VMEM((ts + CARRY_ROWS, CONV_CH), F32),
            pltpu.VMEM((ts // WINDOW, N_KV_HEADS, 2 * WINDOW, LANES), BF16),
            pltpu.VMEM((ts, ATTN_W), F32)],
        compiler_params=pltpu.CompilerParams(
            dimension_semantics=("arbitrary", "arbitrary"),
            vmem_limit_bytes=VMEM_LIMIT_BYTES),
        name="token_mix",
    )(sinks, x, cos, sin, gmix, w_in, conv_w, gconv, gattn, w_out, gx, wq, kt, v, wo)


def _memkv_kernel(mem_ref, g_ref, wkv_ref, kt_ref, v_ref):
    nb, m, d = mem_ref.shape
    g = g_ref[pl.ds(pl.program_id(0), 1), :]
    memn = _rms(mem_ref[...].reshape(nb * m, d), g).astype(BF16)
    kv = _dot(memn, wkv_ref[...])
    for b in range(nb):
        rows = slice(b * m, (b + 1) * m)
        kt_ref[b] = kv[rows, 0:d].T.astype(BF16)
        v_ref[b] = kv[rows, d:2 * d].astype(BF16)


def _mem_kv(mem, norm_mem_g, wx_kv):
    b, m, d = mem.shape
    depth = wx_kv.shape[0]
    return pl.pallas_call(
        _memkv_kernel,
        grid=(depth,),
        in_specs=[pl.BlockSpec((b, m, d), lambda l: (0, 0, 0)),
                  pl.BlockSpec((depth, d), lambda l: (0, 0)),
                  pl.BlockSpec((None, d, 2 * d), lambda l: (l, 0, 0))],
        out_specs=[pl.BlockSpec((None, b, d, m), lambda l: (l, 0, 0, 0)),
                   pl.BlockSpec((None, b, m, d), lambda l: (l, 0, 0, 0))],
        out_shape=[jax.ShapeDtypeStruct((depth, b, d, m), BF16),
                   jax.ShapeDtypeStruct((depth, b, m, d), BF16)],
        compiler_params=pltpu.CompilerParams(vmem_limit_bytes=VMEM_LIMIT_BYTES),
        name="mem_kv",
    )(mem, norm_mem_g, wx_kv)


def _mlp_kernel(x_ref, g_ref, wup_ref, wdn_ref, *rest, layer, final):
    o_ref = rest[-1]
    x = x_ref[...]
    h = _rms(x, g_ref[layer:layer + 1, :]).astype(BF16)
    acc = x
    for c in range(D_FF // FF_CHUNK):
        sl = slice(c * FF_CHUNK, (c + 1) * FF_CHUNK)
        up = _dot(h, wup_ref[:, sl])
        act = jnp.square(jnp.maximum(up, 0.0)).astype(BF16)
        acc = acc + _dot(act, wdn_ref[sl, :])
    if final:
        acc = _rms(acc, rest[0][...])
    o_ref[...] = acc


def _mlp_layer(layer, x2d, g, w_up, w_down, final_g):
    n, d = x2d.shape
    tm = MLP_TILE
    final = final_g is not None
    tok = pl.BlockSpec((tm, d), lambda i: (i, 0))
    per_layer = lambda *shape: _resident((None,) + shape, lambda i: (layer,) + (0,) * len(shape))
    in_specs = [tok, _resident(g.shape, lambda i: (0, 0)), per_layer(d, D_FF), per_layer(D_FF, d)]
    args = [x2d, g, w_up, w_down]
    if final:
        in_specs.append(_resident((1, d), lambda i: (0, 0)))
        args.append(final_g.reshape(1, d))
    return pl.pallas_call(
        functools.partial(_mlp_kernel, layer=layer, final=final),
        grid=(n // tm,),
        in_specs=in_specs,
        out_specs=tok,
        out_shape=jax.ShapeDtypeStruct((n, d), F32),
        compiler_params=pltpu.CompilerParams(vmem_limit_bytes=VMEM_LIMIT_BYTES),
        name="relu2_mlp",
    )(*args)


def kernel(x, mem, positions, norm_mix_g, w_in, conv_w, sinks, gnorm_conv_g, gnorm_attn_g, w_out, norm_x_g, norm_mem_g, wx_q, wx_kv, wx_o, norm_mlp_g, w_up, w_down, final_g):
    b, s, d = x.shape
    depth = w_in.shape[0]
    cos, sin = _rope_tables(positions)
    kt, v = _mem_kv(mem, norm_mem_g, wx_kv)
    for l in range(depth):
        x = _mix_layer(l, x, cos, sin, sinks, norm_mix_g, w_in, conv_w,
                       gnorm_conv_g, gnorm_attn_g, w_out,
                       norm_x_g, wx_q, kt, v, wx_o)
        x = _mlp_layer(l, x.reshape(b * s, d), norm_mlp_g, w_up, w_down,
                       final_g if l == depth - 1 else None).reshape(b, s, d)
    return x
```

```python
import functools
import math

import jax
import jax.numpy as jnp
from jax import lax
from jax.experimental import pallas as pl
from jax.experimental.pallas import tpu as pltpu

D_MODEL = 1024
N_MEM = 256
CONV_CH = 512
CONV_K = 3
HEAD_DIM = 64
N_Q_HEADS = 8
N_KV_HEADS = 2
ATTN_W = N_Q_HEADS * HEAD_DIM
KV_W = N_KV_HEADS * HEAD_DIM
WINDOW = 128
ROPE_THETA = 10000.0
N_X_HEADS = 4
X_HEAD_DIM = D_MODEL // N_X_HEADS
D_FF = 4 * D_MODEL
EPS = 1e-6
LOG2E = math.log2(math.e)
IN_COLS = 3 * CONV_CH + ATTN_W + 2 * KV_W
Q_OFF = 3 * CONV_CH
K_OFF = Q_OFF + ATTN_W

LANES = 128
SUBLANES = 8
V7X_VMEM_BYTES = 64 * 1024 * 1024
VMEM_LIMIT_BYTES = V7X_VMEM_BYTES - 6 * 1024 * 1024

SEQ_TILE = 1024
SUB_TILE = 256
FRONT_AHEAD = 3
MLP_TILE = 1024
FF_CHUNK = 512
CARRY_ROWS = SUBLANES

BF16 = jnp.bfloat16
F32 = jnp.float32


def _rms(x, g):
    ms = jnp.mean(x * x, axis=-1, keepdims=True)
    return x * lax.rsqrt(ms + EPS) * g


def _dot(a, b):
    return jnp.dot(a, b.astype(BF16), preferred_element_type=F32)


def _resident(shape, index_map):
    return pl.BlockSpec(shape, index_map, pipeline_mode=pl.Buffered(1))


def _rope_kernel(pos_ref, inv_ref, cos_ref, sin_ref):
    pos = pos_ref[pl.ds(pl.program_id(0), 1), :]
    ang = inv_ref[...] * pos.astype(F32)
    c = jnp.cos(ang)
    s = jnp.sin(ang)
    cos_ref[0] = jnp.concatenate([c, c, c, c], axis=0).T
    sin_ref[0] = jnp.concatenate([-s, s, -s, s], axis=0).T


def _rope_tables(positions):
    b, s = positions.shape
    half = HEAD_DIM // 2
    inv_freq = ROPE_THETA ** (-jnp.arange(0, HEAD_DIM, 2, dtype=F32) / HEAD_DIM)
    table = pl.BlockSpec((1, s, LANES), lambda i: (i, 0, 0))
    return pl.pallas_call(
        _rope_kernel,
        grid=(b,),
        in_specs=[pl.BlockSpec((b, s), lambda i: (0, 0)),
                  pl.BlockSpec((half, 1), lambda i: (0, 0))],
        out_specs=[table, table],
        out_shape=[jax.ShapeDtypeStruct((b, s, LANES), F32)] * 2,
        name="rope_tables",
    )(positions, inv_freq.reshape(half, 1))


def _mix_kernel(sinks_ref, x_ref, cos_ref, sin_ref, gmix_ref, win_ref, convw_ref,
                gconv_ref, gattn_ref, wout_ref, gx_ref, wq_ref, kt_ref, v_ref, wo_ref, o_ref,
                ke0, ko0, ke1, ko1, ve0, vo0, ve1, vo1, cbuf, qbuf, abuf, *, layer):
    ts = SEQ_TILE
    kbufs = ((ke0, ko0), (ke1, ko1))
    vbufs = ((ve0, vo0), (ve1, vo1))
    kv_all = (ke0, ko0, ke1, ko1, ve0, vo0, ve1, vo1)
    seq_start = pl.program_id(1) == 0

    @pl.when(seq_start)
    def _():
        for buf in kv_all:
            buf[0:WINDOW, :] = jnp.zeros((WINDOW, LANES), BF16)
        cbuf[0:CARRY_ROWS, :] = jnp.zeros((CARRY_ROWS, CONV_CH), F32)

    half_dim = HEAD_DIM // 2
    scale = LOG2E / math.sqrt(HEAD_DIM)
    w = convw_ref[...]

    qi = lax.broadcasted_iota(jnp.int32, (WINDOW, WINDOW), 0)
    ki = lax.broadcasted_iota(jnp.int32, (WINDOW, WINDOW), 1)
    own = ki <= qi
    no_prev = ki < WINDOW * seq_start.astype(jnp.int32)

    carried = {}

    def front(r0):
        n = SUB_TILE
        x = x_ref[0, r0:r0 + n, :]
        h = _rms(x, gmix_ref[layer:layer + 1, :]).astype(BF16)

        cosf = cos_ref[0, r0:r0 + n, :]
        lane = lax.broadcasted_iota(jnp.int32, (n, LANES), 1)
        first_half = (lane & half_dim) == 0
        low_head = lane < HEAD_DIM
        sins = sin_ref[0, r0:r0 + n, :]

        def rope(t):
            rot = jnp.where(first_half,
                            pltpu.roll(t, LANES - half_dim, 1),
                            pltpu.roll(t, half_dim, 1))
            return t * cosf + rot * sins

        uq = _dot(h, win_ref[:, Q_OFF:K_OFF])
        for c in range(ATTN_W // LANES):
            qc = (rope(uq[:, c * LANES:(c + 1) * LANES]) * scale).astype(BF16)
            for jb in range(n // WINDOW):
                qbuf[r0 // WINDOW + jb, c // 2, (c % 2) * WINDOW:(c % 2 + 1) * WINDOW, :] = (
                    qc[jb * WINDOW:(jb + 1) * WINDOW])

        ukv = _dot(h, win_ref[:, K_OFF:IN_COLS])
        k = rope(ukv[:, 0:KV_W])
        v = ukv[:, KV_W:2 * KV_W]
        k0 = WINDOW + r0
        for src, bufs in ((k, kbufs), (v, vbufs)):
            e0 = jnp.where(low_head, src, 0.0)
            o1 = jnp.where(low_head, 0.0, src)
            bufs[0][0][k0:k0 + n, :] = e0.astype(BF16)
            bufs[0][1][k0:k0 + n, :] = pltpu.roll(e0, HEAD_DIM, 1).astype(BF16)
            bufs[1][0][k0:k0 + n, :] = pltpu.roll(o1, HEAD_DIM, 1).astype(BF16)
            bufs[1][1][k0:k0 + n, :] = o1.astype(BF16)

        uc = _dot(h, win_ref[:, 0:Q_OFF])
        cx = uc[:, CONV_CH:2 * CONV_CH] * uc[:, 2 * CONV_CH:3 * CONV_CH]
        c0 = CARRY_ROWS + r0
        cbuf[c0:c0 + n, :] = cx
        y = (w[2:3, :] * cx
             + w[1:2, :] * cbuf[c0 - 1:c0 - 1 + n, :]
             + w[0:1, :] * cbuf[c0 - 2:c0 - 2 + n, :])
        conv_n = _rms(uc[:, 0:CONV_CH] * y, gconv_ref[layer:layer + 1, :]).astype(BF16)
        carried[r0] = (x, conv_n)

    def back(r0):
        n = SUB_TILE
        x, conv_n = carried.pop(r0)
        for jb in range(n // WINDOW):
            j = r0 // WINDOW + jb
            w0 = j * WINDOW
            for g in range(N_KV_HEADS):
                qg = qbuf[j, g]
                o = None
                for par in range(2):
                    s = lax.dot_general(qg, kbufs[g][par][w0:w0 + 2 * WINDOW, :],
                                        (((1,), (1,)), ((), ())),
                                        preferred_element_type=F32)
                    ps, inv_l = [], []
                    for sub in range(2):
                        sink = sinks_ref[layer, 4 * g + 2 * sub + par] * LOG2E
                        s_prev = s[sub * WINDOW:(sub + 1) * WINDOW, 0:WINDOW]
                        s_own = s[sub * WINDOW:(sub + 1) * WINDOW, WINDOW:2 * WINDOW]
                        if j == 0:
                            s_prev = jnp.where(no_prev, -jnp.inf, s_prev)
                        sh = jnp.where(own, s_own, s_prev)
                        m = jnp.maximum(jnp.max(sh, axis=-1, keepdims=True), sink)
                        p = jnp.exp2(sh - m)
                        l = jnp.sum(p, axis=-1, keepdims=True) + jnp.exp2(sink - m)
                        p_own = jnp.where(own, p, 0.0)
                        ps.append(jnp.concatenate([p - p_own, p_own], axis=1).astype(BF16))
                        inv_l.append(1.0 / l)
                    pv = _dot(jnp.concatenate(ps, axis=0), vbufs[g][par][w0:w0 + 2 * WINDOW, :])
                    o_par = [pv[sub * WINDOW:(sub + 1) * WINDOW] * inv_l[sub] for sub in range(2)]
                    o = o_par if o is None else [a + b for a, b in zip(o, o_par)]
                for sub in range(2):
                    c = 2 * g + sub
                    abuf[j * WINDOW:(j + 1) * WINDOW, c * LANES:(c + 1) * LANES] = o[sub]

        attn_n = _rms(abuf[r0:r0 + n, :], gattn_ref[layer:layer + 1, :]).astype(BF16)
        return (x + _dot(conv_n, wout_ref[0:CONV_CH, :])
                + _dot(attn_n, wout_ref[CONV_CH:CONV_CH + ATTN_W, :]))

    def xattn(r0, x1):
        n = SUB_TILE
        h2 = _rms(x1, gx_ref[layer:layer + 1, :]).astype(BF16)
        xscale = 1.0 / math.sqrt(X_HEAD_DIM)
        q = (_dot(h2, wq_ref[...]) * xscale).astype(BF16)
        outs = []
        for hd in range(N_X_HEADS):
            sl = slice(hd * X_HEAD_DIM, (hd + 1) * X_HEAD_DIM)
            s = _dot(q[:, sl], kt_ref[sl, :])
            m = jnp.max(s, axis=-1, keepdims=True)
            p = jnp.exp(s - m)
            l = jnp.sum(p, axis=-1, keepdims=True)
            outs.append((_dot(p.astype(BF16), v_ref[:, sl]) * (1.0 / l)).astype(BF16))
        o_ref[0, r0:r0 + n, :] = x1 + _dot(jnp.concatenate(outs, axis=1), wo_ref[...])

    starts = list(range(0, ts, SUB_TILE))
    for r0 in starts[:FRONT_AHEAD]:
        front(r0)
    x1 = {}
    for i, r0 in enumerate(starts):
        if i + FRONT_AHEAD < len(starts):
            front(starts[i + FRONT_AHEAD])
        x1[i] = back(r0)
        if i >= 1:
            xattn(starts[i - 1], x1.pop(i - 1))
    xattn(starts[-1], x1.pop(len(starts) - 1))

    cbuf[0:CARRY_ROWS, :] = cbuf[ts:ts + CARRY_ROWS, :]
    for buf in kv_all:
        buf[0:WINDOW, :] = buf[ts:ts + WINDOW, :]


def _mix_layer(layer, x, cos, sin, sinks, gmix, w_in, conv_w, gconv, gattn, w_out, gx, wq, kt, v, wo):
    b, s, d = x.shape
    ts = SEQ_TILE
    tok = lambda w: pl.BlockSpec((1, ts, w), lambda i, j: (i, j, 0))
    per_layer = lambda *shape: _resident((None,) + shape, lambda i, j: (layer,) + (0,) * len(shape))
    gains = lambda w: _resident((gmix.shape[0], w), lambda i, j: (0, 0))
    kv_scratch = [pltpu.VMEM((ts + WINDOW, LANES), BF16)] * 8
    return pl.pallas_call(
        functools.partial(_mix_kernel, layer=layer),
        grid=(b, s // ts),
        in_specs=[pl.BlockSpec(memory_space=pltpu.SMEM),
                  tok(d), tok(LANES), tok(LANES),
                  gains(d), per_layer(d, IN_COLS), per_layer(CONV_K, CONV_CH),
                  gains(CONV_CH), gains(ATTN_W), per_layer(CONV_CH + ATTN_W, d),
                  gains(d), per_layer(d, d),
                  pl.BlockSpec((None, None, d, N_MEM), lambda i, j: (layer, i, 0, 0)),
                  pl.BlockSpec((None, None, N_MEM, d), lambda i, j: (layer, i, 0, 0)),
                  per_layer(d, d)],
        out_specs=tok(d),
        out_shape=jax.ShapeDtypeStruct((b, s, d), F32),
        scratch_shapes=kv_scratch + [
            pltpu.VMEM((ts + CARRY_ROWS, CONV_CH), F32),
            pltpu.VMEM((ts // WINDOW, N_KV_HEADS, 2 * WINDOW, LANES), BF16),
            pltpu.VMEM((ts, ATTN_W), F32)],
        compiler_params=pltpu.CompilerParams(
            dimension_semantics=("arbitrary", "arbitrary"),
            vmem_limit_bytes=VMEM_LIMIT_BYTES),
        name="token_mix",
    )(sinks, x, cos, sin, gmix, w_in, conv_w, gconv, gattn, w_out, gx, wq, kt, v, wo)


def _memkv_kernel(mem_ref, g_ref, wkv_ref, kt_ref, v_ref):
    nb, m, d = mem_ref.shape
    g = g_ref[pl.ds(pl.program_id(0), 1), :]
    memn = _rms(mem_ref[...].reshape(nb * m, d), g).astype(BF16)
    kv = _dot(memn, wkv_ref[...])
    for b in range(nb):
        rows = slice(b * m, (b + 1) * m)
        kt_ref[b] = kv[rows, 0:d].T.astype(BF16)
        v_ref[b] = kv[rows, d:2 * d].astype(BF16)


def _mem_kv(mem, norm_mem_g, wx_kv):
    b, m, d = mem.shape
    depth = wx_kv.shape[0]
    return pl.pallas_call(
        _memkv_kernel,
        grid=(depth,),
        in_specs=[pl.BlockSpec((b, m, d), lambda l: (0, 0, 0)),
                  pl.BlockSpec((depth, d), lambda l: (0, 0)),
                  pl.BlockSpec((None, d, 2 * d), lambda l: (l, 0, 0))],
        out_specs=[pl.BlockSpec((None, b, d, m), lambda l: (l, 0, 0, 0)),
                   pl.BlockSpec((None, b, m, d), lambda l: (l, 0, 0, 0))],
        out_shape=[jax.ShapeDtypeStruct((depth, b, d, m), BF16),
                   jax.ShapeDtypeStruct((depth, b, m, d), BF16)],
        compiler_params=pltpu.CompilerParams(vmem_limit_bytes=VMEM_LIMIT_BYTES),
        name="mem_kv",
    )(mem, norm_mem_g, wx_kv)


def _mlp_kernel(x_ref, g_ref, wup_ref, wdn_ref, *rest, layer, final):
    o_ref = rest[-1]
    x = x_ref[...]
    h = _rms(x, g_ref[layer:layer + 1, :]).astype(BF16)
    acc = x
    for c in range(D_FF // FF_CHUNK):
        sl = slice(c * FF_CHUNK, (c + 1) * FF_CHUNK)
        up = _dot(h, wup_ref[:, sl])
        act = jnp.square(jnp.maximum(up, 0.0)).astype(BF16)
        acc = acc + _dot(act, wdn_ref[sl, :])
    if final:
        acc = _rms(acc, rest[0][...])
    o_ref[...] = acc


def _mlp_layer(layer, x2d, g, w_up, w_down, final_g):
    n, d = x2d.shape
    tm = MLP_TILE
    final = final_g is not None
    tok = pl.BlockSpec((tm, d), lambda i: (i, 0))
    per_layer = lambda *shape: _resident((None,) + shape, lambda i: (layer,) + (0,) * len(shape))
    in_specs = [tok, _resident(g.shape, lambda i: (0, 0)), per_layer(d, D_FF), per_layer(D_FF, d)]
    args = [x2d, g, w_up, w_down]
    if final:
        in_specs.append(_resident((1, d), lambda i: (0, 0)))
        args.append(final_g.reshape(1, d))
    return pl.pallas_call(
        functools.partial(_mlp_kernel, layer=layer, final=final),
        grid=(n // tm,),
        in_specs=in_specs,
        out_specs=tok,
        out_shape=jax.ShapeDtypeStruct((n, d), F32),
        compiler_params=pltpu.CompilerParams(vmem_limit_bytes=VMEM_LIMIT_BYTES),
        name="relu2_mlp",
    )(*args)


def kernel(x, mem, positions, norm_mix_g, w_in, conv_w, sinks, gnorm_conv_g, gnorm_attn_g, w_out, norm_x_g, norm_mem_g, wx_q, wx_kv, wx_o, norm_mlp_g, w_up, w_down, final_g):
    b, s, d = x.shape
    depth = w_in.shape[0]
    cos, sin = _rope_tables(positions)
    kt, v = _mem_kv(mem, norm_mem_g, wx_kv)
    for l in range(depth):
        x = _mix_layer(l, x, cos, sin, sinks, norm_mix_g, w_in, conv_w,
                       gnorm_conv_g, gnorm_attn_g, w_out,
                       norm_x_g, wx_q, kt, v, wx_o)
        x = _mlp_layer(l, x.reshape(b * s, d), norm_mlp_g, w_up, w_down,
                       final_g if l == depth - 1 else None).reshape(b, s, d)
    return x
```

```python
import functools
import math

import jax
import jax.numpy as jnp
from jax import lax
from jax.experimental import pallas as pl
from jax.experimental.pallas import tpu as pltpu

D_MODEL = 1024
N_MEM = 256
CONV_CH = 512
CONV_K = 3
HEAD_DIM = 64
N_Q_HEADS = 8
N_KV_HEADS = 2
ATTN_W = N_Q_HEADS * HEAD_DIM
KV_W = N_KV_HEADS * HEAD_DIM
WINDOW = 128
ROPE_THETA = 10000.0
N_X_HEADS = 4
X_HEAD_DIM = D_MODEL // N_X_HEADS
D_FF = 4 * D_MODEL
EPS = 1e-6
LOG2E = math.log2(math.e)
IN_COLS = 3 * CONV_CH + ATTN_W + 2 * KV_W
Q_OFF = 3 * CONV_CH
K_OFF = Q_OFF + ATTN_W

LANES = 128
SUBLANES = 8
V7X_VMEM_BYTES = 64 * 1024 * 1024
VMEM_LIMIT_BYTES = V7X_VMEM_BYTES - 6 * 1024 * 1024

SEQ_TILE = 1024
SUB_TILE = 256
FRONT_ROWS = 512
MLP_TILE = 1024
FF_CHUNK = 512
CARRY_ROWS = SUBLANES

BF16 = jnp.bfloat16
F32 = jnp.float32


def _rms(x, g):
    ms = jnp.mean(x * x, axis=-1, keepdims=True)
    return x * lax.rsqrt(ms + EPS) * g


def _dot(a, b):
    return jnp.dot(a, b.astype(BF16), preferred_element_type=F32)


def _resident(shape, index_map):
    return pl.BlockSpec(shape, index_map, pipeline_mode=pl.Buffered(1))


def _rope_kernel(pos_ref, inv_ref, cos_ref, sin_ref):
    pos = pos_ref[pl.ds(pl.program_id(0), 1), :]
    ang = inv_ref[...] * pos.astype(F32)
    c = jnp.cos(ang)
    s = jnp.sin(ang)
    cos_ref[0] = jnp.concatenate([c, c, c, c], axis=0).T
    sin_ref[0] = jnp.concatenate([-s, s, -s, s], axis=0).T


def _rope_tables(positions):
    b, s = positions.shape
    half = HEAD_DIM // 2
    inv_freq = ROPE_THETA ** (-jnp.arange(0, HEAD_DIM, 2, dtype=F32) / HEAD_DIM)
    table = pl.BlockSpec((1, s, LANES), lambda i: (i, 0, 0))
    return pl.pallas_call(
        _rope_kernel,
        grid=(b,),
        in_specs=[pl.BlockSpec((b, s), lambda i: (0, 0)),
                  pl.BlockSpec((half, 1), lambda i: (0, 0))],
        out_specs=[table, table],
        out_shape=[jax.ShapeDtypeStruct((b, s, LANES), F32)] * 2,
        name="rope_tables",
    )(positions, inv_freq.reshape(half, 1))


def _mix_kernel(sinks_ref, x_ref, cos_ref, sin_ref, gmix_ref, win_ref, convw_ref,
                gconv_ref, gattn_ref, wout_ref, gx_ref, wq_ref, kt_ref, v_ref, wo_ref, o_ref,
                ke0, ko0, ke1, ko1, ve0, vo0, ve1, vo1, cbuf, qbuf, abuf, *, layer):
    ts = SEQ_TILE
    kbufs = ((ke0, ko0), (ke1, ko1))
    vbufs = ((ve0, vo0), (ve1, vo1))
    kv_all = (ke0, ko0, ke1, ko1, ve0, vo0, ve1, vo1)
    seq_start = pl.program_id(1) == 0

    @pl.when(seq_start)
    def _():
        for buf in kv_all:
            buf[0:WINDOW, :] = jnp.zeros((WINDOW, LANES), BF16)
        cbuf[0:CARRY_ROWS, :] = jnp.zeros((CARRY_ROWS, CONV_CH), F32)

    half_dim = HEAD_DIM // 2
    scale = LOG2E / math.sqrt(HEAD_DIM)
    w = convw_ref[...]

    qi = lax.broadcasted_iota(jnp.int32, (WINDOW, WINDOW), 0)
    ki = lax.broadcasted_iota(jnp.int32, (WINDOW, WINDOW), 1)
    own = ki <= qi
    no_prev = ki < WINDOW * seq_start.astype(jnp.int32)

    carried = {}

    def front(r0):
        n = FRONT_ROWS
        x = x_ref[0, r0:r0 + n, :]
        h = _rms(x, gmix_ref[layer:layer + 1, :]).astype(BF16)

        cosf = cos_ref[0, r0:r0 + n, :]
        lane = lax.broadcasted_iota(jnp.int32, (n, LANES), 1)
        first_half = (lane & half_dim) == 0
        low_head = lane < HEAD_DIM
        sins = sin_ref[0, r0:r0 + n, :]

        def rope(t):
            rot = jnp.where(first_half,
                            pltpu.roll(t, LANES - half_dim, 1),
                            pltpu.roll(t, half_dim, 1))
            return t * cosf + rot * sins

        uq = _dot(h, win_ref[:, Q_OFF:K_OFF])
        for c in range(ATTN_W // LANES):
            qc = (rope(uq[:, c * LANES:(c + 1) * LANES]) * scale).astype(BF16)
            for jb in range(n // WINDOW):
                qbuf[r0 // WINDOW + jb, c // 2, (c % 2) * WINDOW:(c % 2 + 1) * WINDOW, :] = (
                    qc[jb * WINDOW:(jb + 1) * WINDOW])

        ukv = _dot(h, win_ref[:, K_OFF:IN_COLS])
        k = rope(ukv[:, 0:KV_W])
        v = ukv[:, KV_W:2 * KV_W]
        k0 = WINDOW + r0
        for src, bufs in ((k, kbufs), (v, vbufs)):
            e0 = jnp.where(low_head, src, 0.0)
            o1 = jnp.where(low_head, 0.0, src)
            bufs[0][0][k0:k0 + n, :] = e0.astype(BF16)
            bufs[0][1][k0:k0 + n, :] = pltpu.roll(e0, HEAD_DIM, 1).astype(BF16)
            bufs[1][0][k0:k0 + n, :] = pltpu.roll(o1, HEAD_DIM, 1).astype(BF16)
            bufs[1][1][k0:k0 + n, :] = o1.astype(BF16)

        uc = _dot(h, win_ref[:, 0:Q_OFF])
        cx = uc[:, CONV_CH:2 * CONV_CH] * uc[:, 2 * CONV_CH:3 * CONV_CH]
        c0 = CARRY_ROWS + r0
        cbuf[c0:c0 + n, :] = cx
        y = (w[2:3, :] * cx
             + w[1:2, :] * cbuf[c0 - 1:c0 - 1 + n, :]
             + w[0:1, :] * cbuf[c0 - 2:c0 - 2 + n, :])
        conv_n = _rms(uc[:, 0:CONV_CH] * y, gconv_ref[layer:layer + 1, :]).astype(BF16)
        for c0 in range(0, n, SUB_TILE):
            carried[r0 + c0] = (x[c0:c0 + SUB_TILE], conv_n[c0:c0 + SUB_TILE])

    def back(r0):
        n = SUB_TILE
        x, conv_n = carried.pop(r0)
        for jb in range(n // WINDOW):
            j = r0 // WINDOW + jb
            w0 = j * WINDOW
            for g in range(N_KV_HEADS):
                qg = qbuf[j, g]
                o = None
                for par in range(2):
                    s = lax.dot_general(qg, kbufs[g][par][w0:w0 + 2 * WINDOW, :],
                                        (((1,), (1,)), ((), ())),
                                        preferred_element_type=F32)
                    ps, inv_l = [], []
                    for sub in range(2):
                        sink = sinks_ref[layer, 4 * g + 2 * sub + par] * LOG2E
                        s_prev = s[sub * WINDOW:(sub + 1) * WINDOW, 0:WINDOW]
                        s_own = s[sub * WINDOW:(sub + 1) * WINDOW, WINDOW:2 * WINDOW]
                        if j == 0:
                            s_prev = jnp.where(no_prev, -jnp.inf, s_prev)
                        sh = jnp.where(own, s_own, s_prev)
                        m = jnp.maximum(jnp.max(sh, axis=-1, keepdims=True), sink)
                        p = jnp.exp2(sh - m)
                        l = jnp.sum(p, axis=-1, keepdims=True) + jnp.exp2(sink - m)
                        p_own = jnp.where(own, p, 0.0)
                        ps.append(jnp.concatenate([p - p_own, p_own], axis=1).astype(BF16))
                        inv_l.append(1.0 / l)
                    pv = _dot(jnp.concatenate(ps, axis=0), vbufs[g][par][w0:w0 + 2 * WINDOW, :])
                    o_par = [pv[sub * WINDOW:(sub + 1) * WINDOW] * inv_l[sub] for sub in range(2)]
                    o = o_par if o is None else [a + b for a, b in zip(o, o_par)]
                for sub in range(2):
                    c = 2 * g + sub
                    abuf[j * WINDOW:(j + 1) * WINDOW, c * LANES:(c + 1) * LANES] = o[sub]

        attn_n = _rms(abuf[r0:r0 + n, :], gattn_ref[layer:layer + 1, :]).astype(BF16)
        return (x + _dot(conv_n, wout_ref[0:CONV_CH, :])
                + _dot(attn_n, wout_ref[CONV_CH:CONV_CH + ATTN_W, :]))

    def xattn(r0, x1):
        n = SUB_TILE
        h2 = _rms(x1, gx_ref[layer:layer + 1, :]).astype(BF16)
        xscale = 1.0 / math.sqrt(X_HEAD_DIM)
        q = (_dot(h2, wq_ref[...]) * xscale).astype(BF16)
        outs = []
        for hd in range(N_X_HEADS):
            sl = slice(hd * X_HEAD_DIM, (hd + 1) * X_HEAD_DIM)
            s = _dot(q[:, sl], kt_ref[sl, :])
            m = jnp.max(s, axis=-1, keepdims=True)
            p = jnp.exp(s - m)
            l = jnp.sum(p, axis=-1, keepdims=True)
            outs.append((_dot(p.astype(BF16), v_ref[:, sl]) * (1.0 / l)).astype(BF16))
        o_ref[0, r0:r0 + n, :] = x1 + _dot(jnp.concatenate(outs, axis=1), wo_ref[...])

    starts = list(range(0, ts, SUB_TILE))
    for r0 in range(0, ts, FRONT_ROWS):
        front(r0)
    x1 = {}
    for i, r0 in enumerate(starts):
        x1[i] = back(r0)
        if i >= 1:
            xattn(starts[i - 1], x1.pop(i - 1))
    xattn(starts[-1], x1.pop(len(starts) - 1))

    cbuf[0:CARRY_ROWS, :] = cbuf[ts:ts + CARRY_ROWS, :]
    for buf in kv_all:
        buf[0:WINDOW, :] = buf[ts:ts + WINDOW, :]


def _mix_layer(layer, x, cos, sin, sinks, gmix, w_in, conv_w, gconv, gattn, w_out, gx, wq, kt, v, wo):
    b, s, d = x.shape
    ts = SEQ_TILE
    tok = lambda w: pl.BlockSpec((1, ts, w), lambda i, j: (i, j, 0))
    per_layer = lambda *shape: _resident((None,) + shape, lambda i, j: (layer,) + (0,) * len(shape))
    gains = lambda w: _resident((gmix.shape[0], w), lambda i, j: (0, 0))
    kv_scratch = [pltpu.VMEM((ts + WINDOW, LANES), BF16)] * 8
    return pl.pallas_call(
        functools.partial(_mix_kernel, layer=layer),
        grid=(b, s // ts),
        in_specs=[pl.BlockSpec(memory_space=pltpu.SMEM),
                  tok(d), tok(LANES), tok(LANES),
                  gains(d), per_layer(d, IN_COLS), per_layer(CONV_K, CONV_CH),
                  gains(CONV_CH), gains(ATTN_W), per_layer(CONV_CH + ATTN_W, d),
                  gains(d), per_layer(d, d),
                  pl.BlockSpec((None, None, d, N_MEM), lambda i, j: (layer, i, 0, 0)),
                  pl.BlockSpec((None, None, N_MEM, d), lambda i, j: (layer, i, 0, 0)),
                  per_layer(d, d)],
        out_specs=tok(d),
        out_shape=jax.ShapeDtypeStruct((b, s, d), F32),
        scratch_shapes=kv_scratch + [
            pltpu.VMEM((ts + CARRY_ROWS, CONV_CH), F32),
            pltpu.VMEM((ts // WINDOW, N_KV_HEADS, 2 * WINDOW, LANES), BF16),
            pltpu.VMEM((ts, ATTN_W), F32)],
        compiler_params=pltpu.CompilerParams(
            dimension_semantics=("arbitrary", "arbitrary"),
            vmem_limit_bytes=VMEM_LIMIT_BYTES),
        name="token_mix",
    )(sinks, x, cos, sin, gmix, w_in, conv_w, gconv, gattn, w_out, gx, wq, kt, v, wo)


def _memkv_kernel(mem_ref, g_ref, wkv_ref, kt_ref, v_ref):
    nb, m, d = mem_ref.shape
    g = g_ref[pl.ds(pl.program_id(0), 1), :]
    memn = _rms(mem_ref[...].reshape(nb * m, d), g).astype(BF16)
    kv = _dot(memn, wkv_ref[...])
    for b in range(nb):
        rows = slice(b * m, (b + 1) * m)
        kt_ref[b] = kv[rows, 0:d].T.astype(BF16)
        v_ref[b] = kv[rows, d:2 * d].astype(BF16)


def _mem_kv(mem, norm_mem_g, wx_kv):
    b, m, d = mem.shape
    depth = wx_kv.shape[0]
    return pl.pallas_call(
        _memkv_kernel,
        grid=(depth,),
        in_specs=[pl.BlockSpec((b, m, d), lambda l: (0, 0, 0)),
                  pl.BlockSpec((depth, d), lambda l: (0, 0)),
                  pl.BlockSpec((None, d, 2 * d), lambda l: (l, 0, 0))],
        out_specs=[pl.BlockSpec((None, b, d, m), lambda l: (l, 0, 0, 0)),
                   pl.BlockSpec((None, b, m, d), lambda l: (l, 0, 0, 0))],
        out_shape=[jax.ShapeDtypeStruct((depth, b, d, m), BF16),
                   jax.ShapeDtypeStruct((depth, b, m, d), BF16)],
        compiler_params=pltpu.CompilerParams(vmem_limit_bytes=VMEM_LIMIT_BYTES),
        name="mem_kv",
    )(mem, norm_mem_g, wx_kv)


def _mlp_kernel(x_ref, g_ref, wup_ref, wdn_ref, *rest, layer, final):
    o_ref = rest[-1]
    x = x_ref[...]
    h = _rms(x, g_ref[layer:layer + 1, :]).astype(BF16)
    acc = x
    for c in range(D_FF // FF_CHUNK):
        sl = slice(c * FF_CHUNK, (c + 1) * FF_CHUNK)
        up = _dot(h, wup_ref[:, sl])
        act = jnp.square(jnp.maximum(up, 0.0)).astype(BF16)
        acc = acc + _dot(act, wdn_ref[sl, :])
    if final:
        acc = _rms(acc, rest[0][...])
    o_ref[...] = acc


def _mlp_layer(layer, x2d, g, w_up, w_down, final_g):
    n, d = x2d.shape
    tm = MLP_TILE
    final = final_g is not None
    tok = pl.BlockSpec((tm, d), lambda i: (i, 0))
    per_layer = lambda *shape: _resident((None,) + shape, lambda i: (layer,) + (0,) * len(shape))
    in_specs = [tok, _resident(g.shape, lambda i: (0, 0)), per_layer(d, D_FF), per_layer(D_FF, d)]
    args = [x2d, g, w_up, w_down]
    if final:
        in_specs.append(_resident((1, d), lambda i: (0, 0)))
        args.append(final_g.reshape(1, d))
    return pl.pallas_call(
        functools.partial(_mlp_kernel, layer=layer, final=final),
        grid=(n // tm,),
        in_specs=in_specs,
        out_specs=tok,
        out_shape=jax.ShapeDtypeStruct((n, d), F32),
        compiler_params=pltpu.CompilerParams(vmem_limit_bytes=VMEM_LIMIT_BYTES),
        name="relu2_mlp",
    )(*args)


def kernel(x, mem, positions, norm_mix_g, w_in, conv_w, sinks, gnorm_conv_g, gnorm_attn_g, w_out, norm_x_g, norm_mem_g, wx_q, wx_kv, wx_o, norm_mlp_g, w_up, w_down, final_g):
    b, s, d = x.shape
    depth = w_in.shape[0]
    cos, sin = _rope_tables(positions)
    kt, v = _mem_kv(mem, norm_mem_g, wx_kv)
    for l in range(depth):
        x = _mix_layer(l, x, cos, sin, sinks, norm_mix_g, w_in, conv_w,
                       gnorm_conv_g, gnorm_attn_g, w_out,
                       norm_x_g, wx_q, kt, v, wx_o)
        x = _mlp_layer(l, x.reshape(b * s, d), norm_mlp_g, w_up, w_down,
                       final_g if l == depth - 1 else None).reshape(b, s, d)
    return x
```

```python
import functools
import math

import jax
import jax.numpy as jnp
from jax import lax
from jax.experimental import pallas as pl
from jax.experimental.pallas import tpu as pltpu

D_MODEL = 1024
N_MEM = 256
CONV_CH = 512
CONV_K = 3
HEAD_DIM = 64
N_Q_HEADS = 8
N_KV_HEADS = 2
ATTN_W = N_Q_HEADS * HEAD_DIM
KV_W = N_KV_HEADS * HEAD_DIM
WINDOW = 128
ROPE_THETA = 10000.0
N_X_HEADS = 4
X_HEAD_DIM = D_MODEL // N_X_HEADS
D_FF = 4 * D_MODEL
EPS = 1e-6
LOG2E = math.log2(math.e)
IN_COLS = 3 * CONV_CH + ATTN_W + 2 * KV_W
Q_OFF = 3 * CONV_CH
K_OFF = Q_OFF + ATTN_W

LANES = 128
SUBLANES = 8
V7X_VMEM_BYTES = 64 * 1024 * 1024
VMEM_LIMIT_BYTES = V7X_VMEM_BYTES - 6 * 1024 * 1024

SEQ_TILE = 1024
SUB_TILE = 256
FRONT_ROWS = 512
MLP_TILE = 1024
FF_CHUNK = 512
CARRY_ROWS = SUBLANES

BF16 = jnp.bfloat16
F32 = jnp.float32


def _rms(x, g):
    ms = jnp.mean(x * x, axis=-1, keepdims=True)
    return x * lax.rsqrt(ms + EPS) * g


def _dot(a, b):
    return jnp.dot(a, b.astype(BF16), preferred_element_type=F32)


def _resident(shape, index_map):
    return pl.BlockSpec(shape, index_map, pipeline_mode=pl.Buffered(1))


def _rope_kernel(pos_ref, inv_ref, cos_ref, sin_ref):
    pos = pos_ref[pl.ds(pl.program_id(0), 1), :]
    ang = inv_ref[...] * pos.astype(F32)
    c = jnp.cos(ang)
    s = jnp.sin(ang)
    cos_ref[0] = jnp.concatenate([c, c, c, c], axis=0).T
    sin_ref[0] = jnp.concatenate([-s, s, -s, s], axis=0).T


def _rope_tables(positions):
    b, s = positions.shape
    half = HEAD_DIM // 2
    inv_freq = ROPE_THETA ** (-jnp.arange(0, HEAD_DIM, 2, dtype=F32) / HEAD_DIM)
    table = pl.BlockSpec((1, s, LANES), lambda i: (i, 0, 0))
    return pl.pallas_call(
        _rope_kernel,
        grid=(b,),
        in_specs=[pl.BlockSpec((b, s), lambda i: (0, 0)),
                  pl.BlockSpec((half, 1), lambda i: (0, 0))],
        out_specs=[table, table],
        out_shape=[jax.ShapeDtypeStruct((b, s, LANES), F32)] * 2,
        name="rope_tables",
    )(positions, inv_freq.reshape(half, 1))


def _mix_kernel(sinks_ref, x_ref, cos_ref, sin_ref, gmix_ref, win_ref, convw_ref,
                gconv_ref, gattn_ref, wout_ref, gx_ref, wq_ref, kt_ref, v_ref, wo_ref, o_ref,
                ke0, ko0, ke1, ko1, ve0, vo0, ve1, vo1, cbuf, qbuf, abuf, *, layer):
    ts = SEQ_TILE
    kbufs = ((ke0, ko0), (ke1, ko1))
    vbufs = ((ve0, vo0), (ve1, vo1))
    kv_all = (ke0, ko0, ke1, ko1, ve0, vo0, ve1, vo1)
    seq_start = pl.program_id(1) == 0

    @pl.when(seq_start)
    def _():
        for buf in kv_all:
            buf[0:WINDOW, :] = jnp.zeros((WINDOW, LANES), BF16)
        cbuf[0:CARRY_ROWS, :] = jnp.zeros((CARRY_ROWS, CONV_CH), F32)

    half_dim = HEAD_DIM // 2
    scale = LOG2E / math.sqrt(HEAD_DIM)
    w = convw_ref[...]

    qi = lax.broadcasted_iota(jnp.int32, (WINDOW, WINDOW), 0)
    ki = lax.broadcasted_iota(jnp.int32, (WINDOW, WINDOW), 1)
    own = ki <= qi
    no_prev = ki < WINDOW * seq_start.astype(jnp.int32)

    carried = {}

    def front(r0):
        n = FRONT_ROWS
        x = x_ref[0, r0:r0 + n, :]
        h = _rms(x, gmix_ref[layer:layer + 1, :]).astype(BF16)

        cosf = cos_ref[0, r0:r0 + n, :]
        lane = lax.broadcasted_iota(jnp.int32, (n, LANES), 1)
        first_half = (lane & half_dim) == 0
        low_head = lane < HEAD_DIM
        sins = sin_ref[0, r0:r0 + n, :]

        def rope(t):
            rot = jnp.where(first_half,
                            pltpu.roll(t, LANES - half_dim, 1),
                            pltpu.roll(t, half_dim, 1))
            return t * cosf + rot * sins

        uq = _dot(h, win_ref[:, Q_OFF:K_OFF])
        for c in range(ATTN_W // LANES):
            qc = (rope(uq[:, c * LANES:(c + 1) * LANES]) * scale).astype(BF16)
            for jb in range(n // WINDOW):
                qbuf[r0 // WINDOW + jb, c // 2, (c % 2) * WINDOW:(c % 2 + 1) * WINDOW, :] = (
                    qc[jb * WINDOW:(jb + 1) * WINDOW])

        ukv = _dot(h, win_ref[:, K_OFF:IN_COLS])
        k = rope(ukv[:, 0:KV_W])
        v = ukv[:, KV_W:2 * KV_W]
        k0 = WINDOW + r0
        for src, bufs in ((k, kbufs), (v, vbufs)):
            e0 = jnp.where(low_head, src, 0.0)
            o1 = jnp.where(low_head, 0.0, src)
            bufs[0][0][k0:k0 + n, :] = e0.astype(BF16)
            bufs[0][1][k0:k0 + n, :] = pltpu.roll(e0, HEAD_DIM, 1).astype(BF16)
            bufs[1][0][k0:k0 + n, :] = pltpu.roll(o1, HEAD_DIM, 1).astype(BF16)
            bufs[1][1][k0:k0 + n, :] = o1.astype(BF16)

        uc = _dot(h, win_ref[:, 0:Q_OFF])
        cx = uc[:, CONV_CH:2 * CONV_CH] * uc[:, 2 * CONV_CH:3 * CONV_CH]
        c0 = CARRY_ROWS + r0
        cbuf[c0:c0 + n, :] = cx
        y = (w[2:3, :] * cx
             + w[1:2, :] * cbuf[c0 - 1:c0 - 1 + n, :]
             + w[0:1, :] * cbuf[c0 - 2:c0 - 2 + n, :])
        conv_n = _rms(uc[:, 0:CONV_CH] * y, gconv_ref[layer:layer + 1, :]).astype(BF16)
        for c0 in range(0, n, SUB_TILE):
            carried[r0 + c0] = (x[c0:c0 + SUB_TILE], conv_n[c0:c0 + SUB_TILE])

    def back(r0):
        n = SUB_TILE
        x, conv_n = carried.pop(r0)
        for jb in range(n // WINDOW):
            j = r0 // WINDOW + jb
            w0 = j * WINDOW
            for g in range(N_KV_HEADS):
                qg = qbuf[j, g]
                o = None
                for par in range(2):
                    s = lax.dot_general(qg, kbufs[g][par][w0:w0 + 2 * WINDOW, :],
                                        (((1,), (1,)), ((), ())),
                                        preferred_element_type=F32)
                    ps, inv_l = [], []
                    for sub in range(2):
                        sink = sinks_ref[layer, 4 * g + 2 * sub + par] * LOG2E
                        s_prev = s[sub * WINDOW:(sub + 1) * WINDOW, 0:WINDOW]
                        s_own = s[sub * WINDOW:(sub + 1) * WINDOW, WINDOW:2 * WINDOW]
                        if j == 0:
                            s_prev = jnp.where(no_prev, -jnp.inf, s_prev)
                        sh = jnp.where(own, s_own, s_prev)
                        m = jnp.maximum(jnp.max(sh, axis=-1, keepdims=True), sink)
                        p = jnp.exp2(sh - m)
                        l = jnp.sum(p, axis=-1, keepdims=True) + jnp.exp2(sink - m)
                        p_own = jnp.where(own, p, 0.0)
                        ps.append(jnp.concatenate([p - p_own, p_own], axis=1).astype(BF16))
                        inv_l.append(1.0 / l)
                    pv = _dot(jnp.concatenate(ps, axis=0), vbufs[g][par][w0:w0 + 2 * WINDOW, :])
                    o_par = [pv[sub * WINDOW:(sub + 1) * WINDOW] * inv_l[sub] for sub in range(2)]
                    o = o_par if o is None else [a + b for a, b in zip(o, o_par)]
                for sub in range(2):
                    c = 2 * g + sub
                    abuf[j * WINDOW:(j + 1) * WINDOW, c * LANES:(c + 1) * LANES] = o[sub]

        attn_n = _rms(abuf[r0:r0 + n, :], gattn_ref[layer:layer + 1, :]).astype(BF16)
        return (x + _dot(conv_n, wout_ref[0:CONV_CH, :])
                + _dot(attn_n, wout_ref[CONV_CH:CONV_CH + ATTN_W, :]))

    def xattn(r0, x1):
        n = SUB_TILE
        h2 = _rms(x1, gx_ref[layer:layer + 1, :]).astype(BF16)
        xscale = 1.0 / math.sqrt(X_HEAD_DIM)
        q = (_dot(h2, wq_ref[...]) * xscale).astype(BF16)
        outs = []
        for hd in range(N_X_HEADS):
            sl = slice(hd * X_HEAD_DIM, (hd + 1) * X_HEAD_DIM)
            s = _dot(q[:, sl], kt_ref[sl, :])
            m = jnp.max(s, axis=-1, keepdims=True)
            p = jnp.exp(s - m)
            l = jnp.sum(p, axis=-1, keepdims=True)
            outs.append((_dot(p.astype(BF16), v_ref[:, sl]) * (1.0 / l)).astype(BF16))
        o_ref[0, r0:r0 + n, :] = x1 + _dot(jnp.concatenate(outs, axis=1), wo_ref[...])

    starts = list(range(0, ts, SUB_TILE))
    for r0 in range(0, ts, FRONT_ROWS):
        front(r0)
    x1 = {}
    for i, r0 in enumerate(starts):
        x1[i] = back(r0)
        if i >= 1:
            xattn(starts[i - 1], x1.pop(i - 1))
    xattn(starts[-1], x1.pop(len(starts) - 1))

    cbuf[0:CARRY_ROWS, :] = cbuf[ts:ts + CARRY_ROWS, :]
    for buf in kv_all:
        buf[0:WINDOW, :] = buf[ts:ts + WINDOW, :]


def _mix_layer(layer, x, cos, sin, sinks, gmix, w_in, conv_w, gconv, gattn, w_out, gx, wq, kt, v, wo):
    b, s, d = x.shape
    ts = SEQ_TILE
    tok = lambda w: pl.BlockSpec((1, ts, w), lambda i, j: (i, j, 0))
    per_layer = lambda *shape: _resident((None,) + shape, lambda i, j: (layer,) + (0,) * len(shape))
    gains = lambda w: _resident((gmix.shape[0], w), lambda i, j: (0, 0))
    kv_scratch = [pltpu.VMEM((ts + WINDOW, LANES), BF16)] * 8
    return pl.pallas_call(
        functools.partial(_mix_kernel, layer=layer),
        grid=(b, s // ts),
        in_specs=[pl.BlockSpec(memory_space=pltpu.SMEM),
                  tok(d), tok(LANES), tok(LANES),
                  gains(d), per_layer(d, IN_COLS), per_layer(CONV_K, CONV_CH),
                  gains(CONV_CH), gains(ATTN_W), per_layer(CONV_CH + ATTN_W, d),
                  gains(d), per_layer(d, d),
                  pl.BlockSpec((None, None, d, N_MEM), lambda i, j: (layer, i, 0, 0)),
                  pl.BlockSpec((None, None, N_MEM, d), lambda i, j: (layer, i, 0, 0)),
                  per_layer(d, d)],
        out_specs=tok(d),
        out_shape=jax.ShapeDtypeStruct((b, s, d), F32),
        scratch_shapes=kv_scratch + [
            pltpu.VMEM((ts + CARRY_ROWS, CONV_CH), F32),
            pltpu.VMEM((ts // WINDOW, N_KV_HEADS, 2 * WINDOW, LANES), BF16),
            pltpu.VMEM((ts, ATTN_W), F32)],
        compiler_params=pltpu.CompilerParams(
            dimension_semantics=("arbitrary", "arbitrary"),
            vmem_limit_bytes=VMEM_LIMIT_BYTES),
        name="token_mix",
    )(sinks, x, cos, sin, gmix, w_in, conv_w, gconv, gattn, w_out, gx, wq, kt, v, wo)


def _memkv_kernel(mem_ref, g_ref, wkv_ref, kt_ref, v_ref):
    nb, m, d = mem_ref.shape
    g = g_ref[pl.ds(pl.program_id(0), 1), :]
    memn = _rms(mem_ref[...].reshape(nb * m, d), g).astype(BF16)
    kv = _dot(memn, wkv_ref[...])
    for b in range(nb):
        rows = slice(b * m, (b + 1) * m)
        kt_ref[b] = kv[rows, 0:d].T.astype(BF16)
        v_ref[b] = kv[rows, d:2 * d].astype(BF16)


def _mem_kv(mem, norm_mem_g, wx_kv):
    b, m, d = mem.shape
    depth = wx_kv.shape[0]
    return pl.pallas_call(
        _memkv_kernel,
        grid=(depth,),
        in_specs=[pl.BlockSpec((b, m, d), lambda l: (0, 0, 0)),
                  pl.BlockSpec((depth, d), lambda l: (0, 0)),
                  pl.BlockSpec((None, d, 2 * d), lambda l: (l, 0, 0))],
        out_specs=[pl.BlockSpec((None, b, d, m), lambda l: (l, 0, 0, 0)),
                   pl.BlockSpec((None, b, m, d), lambda l: (l, 0, 0, 0))],
        out_shape=[jax.ShapeDtypeStruct((depth, b, d, m), BF16),
                   jax.ShapeDtypeStruct((depth, b, m, d), BF16)],
        compiler_params=pltpu.CompilerParams(vmem_limit_bytes=VMEM_LIMIT_BYTES),
        name="mem_kv",
    )(mem, norm_mem_g, wx_kv)


def _mlp_kernel(x_ref, g_ref, wup_ref, wdn_ref, *rest, layer, final):
    o_ref = rest[-1]
    x = x_ref[...]
    h = _rms(x, g_ref[layer:layer + 1, :]).astype(BF16)
    acc = x
    for c in range(D_FF // FF_CHUNK):
        sl = slice(c * FF_CHUNK, (c + 1) * FF_CHUNK)
        up = _dot(h, wup_ref[:, sl])
        act = jnp.square(jnp.maximum(up, 0.0)).astype(BF16)
        acc = acc + _dot(act, wdn_ref[sl, :])
    if final:
        acc = _rms(acc, rest[0][...])
    o_ref[...] = acc


def _mlp_layer(layer, x2d, g, w_up, w_down, final_g):
    n, d = x2d.shape
    tm = MLP_TILE
    final = final_g is not None
    tok = pl.BlockSpec((tm, d), lambda i: (i, 0))
    per_layer = lambda *shape: _resident((None,) + shape, lambda i: (layer,) + (0,) * len(shape))
    in_specs = [tok, _resident(g.shape, lambda i: (0, 0)), per_layer(d, D_FF), per_layer(D_FF, d)]
    args = [x2d, g, w_up, w_down]
    if final:
        in_specs.append(_resident((1, d), lambda i: (0, 0)))
        args.append(final_g.reshape(1, d))
    return pl.pallas_call(
        functools.partial(_mlp_kernel, layer=layer, final=final),
        grid=(n // tm,),
        in_specs=in_specs,
        out_specs=tok,
        out_shape=jax.ShapeDtypeStruct((n, d), F32),
        compiler_params=pltpu.CompilerParams(vmem_limit_bytes=VMEM_LIMIT_BYTES),
        name="relu2_mlp",
    )(*args)


def kernel(x, mem, positions, norm_mix_g, w_in, conv_w, sinks, gnorm_conv_g, gnorm_attn_g, w_out, norm_x_g, norm_mem_g, wx_q, wx_kv, wx_o, norm_mlp_g, w_up, w_down, final_g):
    b, s, d = x.shape
    depth = w_in.shape[0]
    cos, sin = _rope_tables(positions)
    kt, v = _mem_kv(mem, norm_mem_g, wx_kv)
    w_in, w_out, wx_q, wx_o = (t.astype(BF16) for t in (w_in, w_out, wx_q, wx_o))
    for l in range(depth):
        x = _mix_layer(l, x, cos, sin, sinks, norm_mix_g, w_in, conv_w,
                       gnorm_conv_g, gnorm_attn_g, w_out,
                       norm_x_g, wx_q, kt, v, wx_o)
        x = _mlp_layer(l, x.reshape(b * s, d), norm_mlp_g, w_up, w_down,
                       final_g if l == depth - 1 else None).reshape(b, s, d)
    return x
```

```python
import functools
import math

import jax
import jax.numpy as jnp
from jax import lax
from jax.experimental import pallas as pl
from jax.experimental.pallas import tpu as pltpu

D_MODEL = 1024
N_MEM = 256
CONV_CH = 512
CONV_K = 3
HEAD_DIM = 64
N_Q_HEADS = 8
N_KV_HEADS = 2
ATTN_W = N_Q_HEADS * HEAD_DIM
KV_W = N_KV_HEADS * HEAD_DIM
WINDOW = 128
ROPE_THETA = 10000.0
N_X_HEADS = 4
X_HEAD_DIM = D_MODEL // N_X_HEADS
D_FF = 4 * D_MODEL
EPS = 1e-6
LOG2E = math.log2(math.e)
IN_COLS = 3 * CONV_CH + ATTN_W + 2 * KV_W
Q_OFF = 3 * CONV_CH
K_OFF = Q_OFF + ATTN_W

LANES = 128
SUBLANES = 8
V7X_VMEM_BYTES = 64 * 1024 * 1024
VMEM_LIMIT_BYTES = V7X_VMEM_BYTES - 6 * 1024 * 1024

SEQ_TILE = 1024
SUB_TILE = 256
FRONT_ROWS = 512
MLP_TILE = 1024
FF_CHUNK = 512
CARRY_ROWS = SUBLANES

BF16 = jnp.bfloat16
F32 = jnp.float32


def _rms(x, g):
    ms = jnp.mean(x * x, axis=-1, keepdims=True)
    return x * lax.rsqrt(ms + EPS) * g


def _dot(a, b):
    return jnp.dot(a, b.astype(BF16), preferred_element_type=F32)


def _resident(shape, index_map):
    return pl.BlockSpec(shape, index_map, pipeline_mode=pl.Buffered(1))


def _prep_kernel(pos_ref, inv_ref, mem_ref, g_ref, wkv_ref, cos_ref, sin_ref, kt_ref, v_ref,
                 *, groups_per_layer):
    i = pl.program_id(0)
    pos = pos_ref[pl.ds(i, 1), :]
    ang = inv_ref[...] * pos.astype(F32)
    c = jnp.cos(ang)
    s = jnp.sin(ang)
    cos_ref[0] = jnp.concatenate([c, c, c, c], axis=0).T
    sin_ref[0] = jnp.concatenate([-s, s, -s, s], axis=0).T

    nb, m, d = mem_ref.shape
    g = g_ref[pl.ds(i // groups_per_layer, 1), :]
    memn = _rms(mem_ref[...].reshape(nb * m, d), g).astype(BF16)
    kv = _dot(memn, wkv_ref[...])
    for b in range(nb):
        rows = slice(b * m, (b + 1) * m)
        kt_ref[b] = kv[rows, 0:d].T.astype(BF16)
        v_ref[b] = kv[rows, d:2 * d].astype(BF16)


def _prepare(positions, mem, norm_mem_g, wx_kv):
    b, s = positions.shape
    _, m, d = mem.shape
    depth = wx_kv.shape[0]
    assert b % depth == 0
    groups = b // depth
    half = HEAD_DIM // 2
    inv_freq = ROPE_THETA ** (-jnp.arange(0, HEAD_DIM, 2, dtype=F32) / HEAD_DIM)
    table = pl.BlockSpec((1, s, LANES), lambda i: (i, 0, 0))
    return pl.pallas_call(
        functools.partial(_prep_kernel, groups_per_layer=groups),
        grid=(b,),
        in_specs=[pl.BlockSpec((b, s), lambda i: (0, 0)),
                  pl.BlockSpec((half, 1), lambda i: (0, 0)),
                  pl.BlockSpec((depth, m, d), lambda i: (i % groups, 0, 0)),
                  pl.BlockSpec((depth, d), lambda i: (0, 0)),
                  pl.BlockSpec((None, d, 2 * d), lambda i: (i // groups, 0, 0))],
        out_specs=[table, table,
                   pl.BlockSpec((None, depth, d, m), lambda i: (i // groups, i % groups, 0, 0)),
                   pl.BlockSpec((None, depth, m, d), lambda i: (i // groups, i % groups, 0, 0))],
        out_shape=[jax.ShapeDtypeStruct((b, s, LANES), F32)] * 2
        + [jax.ShapeDtypeStruct((depth, b, d, m), BF16),
           jax.ShapeDtypeStruct((depth, b, m, d), BF16)],
        compiler_params=pltpu.CompilerParams(
            dimension_semantics=("arbitrary",),
            vmem_limit_bytes=VMEM_LIMIT_BYTES),
        name="rope_and_mem_kv",
    )(positions, inv_freq.reshape(half, 1), mem, norm_mem_g, wx_kv)


def _mix_kernel(sinks_ref, x_ref, cos_ref, sin_ref, gmix_ref, win_ref, convw_ref,
                gconv_ref, gattn_ref, wout_ref, gx_ref, wq_ref, kt_ref, v_ref, wo_ref, o_ref,
                ke0, ko0, ke1, ko1, ve0, vo0, ve1, vo1, cbuf, qbuf, abuf, *, layer):
    ts = SEQ_TILE
    kbufs = ((ke0, ko0), (ke1, ko1))
    vbufs = ((ve0, vo0), (ve1, vo1))
    kv_all = (ke0, ko0, ke1, ko1, ve0, vo0, ve1, vo1)
    seq_start = pl.program_id(1) == 0

    @pl.when(seq_start)
    def _():
        for buf in kv_all:
            buf[0:WINDOW, :] = jnp.zeros((WINDOW, LANES), BF16)
        cbuf[0:CARRY_ROWS, :] = jnp.zeros((CARRY_ROWS, CONV_CH), F32)

    half_dim = HEAD_DIM // 2
    scale = LOG2E / math.sqrt(HEAD_DIM)
    w = convw_ref[...]

    qi = lax.broadcasted_iota(jnp.int32, (WINDOW, WINDOW), 0)
    ki = lax.broadcasted_iota(jnp.int32, (WINDOW, WINDOW), 1)
    own = ki <= qi
    no_prev = ki < WINDOW * seq_start.astype(jnp.int32)

    carried = {}

    def front(r0):
        n = FRONT_ROWS
        x = x_ref[0, r0:r0 + n, :]
        h = _rms(x, gmix_ref[layer:layer + 1, :]).astype(BF16)

        cosf = cos_ref[0, r0:r0 + n, :]
        lane = lax.broadcasted_iota(jnp.int32, (n, LANES), 1)
        first_half = (lane & half_dim) == 0
        low_head = lane < HEAD_DIM
        sins = sin_ref[0, r0:r0 + n, :]

        def rope(t):
            rot = jnp.where(first_half,
                            pltpu.roll(t, LANES - half_dim, 1),
                            pltpu.roll(t, half_dim, 1))
            return t * cosf + rot * sins

        uq = _dot(h, win_ref[:, Q_OFF:K_OFF])
        for c in range(ATTN_W // LANES):
            qc = (rope(uq[:, c * LANES:(c + 1) * LANES]) * scale).astype(BF16)
            for jb in range(n // WINDOW):
                qbuf[r0 // WINDOW + jb, c // 2, (c % 2) * WINDOW:(c % 2 + 1) * WINDOW, :] = (
                    qc[jb * WINDOW:(jb + 1) * WINDOW])

        ukv = _dot(h, win_ref[:, K_OFF:IN_COLS])
        k = rope(ukv[:, 0:KV_W])
        v = ukv[:, KV_W:2 * KV_W]
        k0 = WINDOW + r0
        for src, bufs in ((k, kbufs), (v, vbufs)):
            e0 = jnp.where(low_head, src, 0.0)
            o1 = jnp.where(low_head, 0.0, src)
            bufs[0][0][k0:k0 + n, :] = e0.astype(BF16)
            bufs[0][1][k0:k0 + n, :] = pltpu.roll(e0, HEAD_DIM, 1).astype(BF16)
            bufs[1][0][k0:k0 + n, :] = pltpu.roll(o1, HEAD_DIM, 1).astype(BF16)
            bufs[1][1][k0:k0 + n, :] = o1.astype(BF16)

        uc = _dot(h, win_ref[:, 0:Q_OFF])
        cx = uc[:, CONV_CH:2 * CONV_CH] * uc[:, 2 * CONV_CH:3 * CONV_CH]
        c0 = CARRY_ROWS + r0
        cbuf[c0:c0 + n, :] = cx
        y = (w[2:3, :] * cx
             + w[1:2, :] * cbuf[c0 - 1:c0 - 1 + n, :]
             + w[0:1, :] * cbuf[c0 - 2:c0 - 2 + n, :])
        conv_n = _rms(uc[:, 0:CONV_CH] * y, gconv_ref[layer:layer + 1, :]).astype(BF16)
        for c0 in range(0, n, SUB_TILE):
            carried[r0 + c0] = (x[c0:c0 + SUB_TILE], conv_n[c0:c0 + SUB_TILE])

    def back(r0):
        n = SUB_TILE
        x, conv_n = carried.pop(r0)
        for jb in range(n // WINDOW):
            j = r0 // WINDOW + jb
            w0 = j * WINDOW
            for g in range(N_KV_HEADS):
                qg = qbuf[j, g]
                o = None
                for par in range(2):
                    s = lax.dot_general(qg, kbufs[g][par][w0:w0 + 2 * WINDOW, :],
                                        (((1,), (1,)), ((), ())),
                                        preferred_element_type=F32)
                    ps, inv_l = [], []
                    for sub in range(2):
                        sink = sinks_ref[layer, 4 * g + 2 * sub + par] * LOG2E
                        s_prev = s[sub * WINDOW:(sub + 1) * WINDOW, 0:WINDOW]
                        s_own = s[sub * WINDOW:(sub + 1) * WINDOW, WINDOW:2 * WINDOW]
                        if j == 0:
                            s_prev = jnp.where(no_prev, -jnp.inf, s_prev)
                        sh = jnp.where(own, s_own, s_prev)
                        m = jnp.maximum(jnp.max(sh, axis=-1, keepdims=True), sink)
                        p = jnp.exp2(sh - m)
                        l = jnp.sum(p, axis=-1, keepdims=True) + jnp.exp2(sink - m)
                        p_own = jnp.where(own, p, 0.0)
                        ps.append(jnp.concatenate([p - p_own, p_own], axis=1).astype(BF16))
                        inv_l.append(1.0 / l)
                    pv = _dot(jnp.concatenate(ps, axis=0), vbufs[g][par][w0:w0 + 2 * WINDOW, :])
                    o_par = [pv[sub * WINDOW:(sub + 1) * WINDOW] * inv_l[sub] for sub in range(2)]
                    o = o_par if o is None else [a + b for a, b in zip(o, o_par)]
                for sub in range(2):
                    c = 2 * g + sub
                    abuf[j * WINDOW:(j + 1) * WINDOW, c * LANES:(c + 1) * LANES] = o[sub]

        attn_n = _rms(abuf[r0:r0 + n, :], gattn_ref[layer:layer + 1, :]).astype(BF16)
        return (x + _dot(conv_n, wout_ref[0:CONV_CH, :])
                + _dot(attn_n, wout_ref[CONV_CH:CONV_CH + ATTN_W, :]))

    def xattn(r0, x1):
        n = SUB_TILE
        h2 = _rms(x1, gx_ref[layer:layer + 1, :]).astype(BF16)
        xscale = 1.0 / math.sqrt(X_HEAD_DIM)
        q = (_dot(h2, wq_ref[...]) * xscale).astype(BF16)
        outs = []
        for hd in range(N_X_HEADS):
            sl = slice(hd * X_HEAD_DIM, (hd + 1) * X_HEAD_DIM)
            s = _dot(q[:, sl], kt_ref[sl, :])
            m = jnp.max(s, axis=-1, keepdims=True)
            p = jnp.exp(s - m)
            l = jnp.sum(p, axis=-1, keepdims=True)
            outs.append((_dot(p.astype(BF16), v_ref[:, sl]) * (1.0 / l)).astype(BF16))
        o_ref[0, r0:r0 + n, :] = x1 + _dot(jnp.concatenate(outs, axis=1), wo_ref[...])

    starts = list(range(0, ts, SUB_TILE))
    for r0 in range(0, ts, FRONT_ROWS):
        front(r0)
    x1 = {}
    for i, r0 in enumerate(starts):
        x1[i] = back(r0)
        if i >= 1:
            xattn(starts[i - 1], x1.pop(i - 1))
    xattn(starts[-1], x1.pop(len(starts) - 1))

    cbuf[0:CARRY_ROWS, :] = cbuf[ts:ts + CARRY_ROWS, :]
    for buf in kv_all:
        buf[0:WINDOW, :] = buf[ts:ts + WINDOW, :]


def _mix_layer(layer, x, cos, sin, sinks, gmix, w_in, conv_w, gconv, gattn, w_out, gx, wq, kt, v, wo):
    b, s, d = x.shape
    ts = SEQ_TILE
    tok = lambda w: pl.BlockSpec((1, ts, w), lambda i, j: (i, j, 0))
    per_layer = lambda *shape: _resident((None,) + shape, lambda i, j: (layer,) + (0,) * len(shape))
    gains = lambda w: _resident((gmix.shape[0], w), lambda i, j: (0, 0))
    kv_scratch = [pltpu.VMEM((ts + WINDOW, LANES), BF16)] * 8
    return pl.pallas_call(
        functools.partial(_mix_kernel, layer=layer),
        grid=(b, s // ts),
        in_specs=[pl.BlockSpec(memory_space=pltpu.SMEM),
                  tok(d), tok(LANES), tok(LANES),
                  gains(d), per_layer(d, IN_COLS), per_layer(CONV_K, CONV_CH),
                  gains(CONV_CH), gains(ATTN_W), per_layer(CONV_CH + ATTN_W, d),
                  gains(d), per_layer(d, d),
                  pl.BlockSpec((None, None, d, N_MEM), lambda i, j: (layer, i, 0, 0)),
                  pl.BlockSpec((None, None, N_MEM, d), lambda i, j: (layer, i, 0, 0)),
                  per_layer(d, d)],
        out_specs=tok(d),
        out_shape=jax.ShapeDtypeStruct((b, s, d), F32),
        scratch_shapes=kv_scratch + [
            pltpu.VMEM((ts + CARRY_ROWS, CONV_CH), F32),
            pltpu.VMEM((ts // WINDOW, N_KV_HEADS, 2 * WINDOW, LANES), BF16),
            pltpu.VMEM((ts, ATTN_W), F32)],
        compiler_params=pltpu.CompilerParams(
            dimension_semantics=("arbitrary", "arbitrary"),
            vmem_limit_bytes=VMEM_LIMIT_BYTES),
        name="token_mix",
    )(sinks, x, cos, sin, gmix, w_in, conv_w, gconv, gattn, w_out, gx, wq, kt, v, wo)


def _mlp_kernel(x_ref, g_ref, wup_ref, wdn_ref, *rest, layer, final):
    o_ref = rest[-1]
    x = x_ref[...]
    h = _rms(x, g_ref[layer:layer + 1, :]).astype(BF16)
    acc = x
    for c in range(D_FF // FF_CHUNK):
        sl = slice(c * FF_CHUNK, (c + 1) * FF_CHUNK)
        up = _dot(h, wup_ref[:, sl])
        act = jnp.square(jnp.maximum(up, 0.0)).astype(BF16)
        acc = acc + _dot(act, wdn_ref[sl, :])
    if final:
        acc = _rms(acc, rest[0][...])
    o_ref[...] = acc


def _mlp_layer(layer, x2d, g, w_up, w_down, final_g):
    n, d = x2d.shape
    tm = MLP_TILE
    final = final_g is not None
    tok = pl.BlockSpec((tm, d), lambda i: (i, 0))
    per_layer = lambda *shape: _resident((None,) + shape, lambda i: (layer,) + (0,) * len(shape))
    in_specs = [tok, _resident(g.shape, lambda i: (0, 0)), per_layer(d, D_FF), per_layer(D_FF, d)]
    args = [x2d, g, w_up, w_down]
    if final:
        in_specs.append(_resident((1, d), lambda i: (0, 0)))
        args.append(final_g.reshape(1, d))
    return pl.pallas_call(
        functools.partial(_mlp_kernel, layer=layer, final=final),
        grid=(n // tm,),
        in_specs=in_specs,
        out_specs=tok,
        out_shape=jax.ShapeDtypeStruct((n, d), F32),
        compiler_params=pltpu.CompilerParams(vmem_limit_bytes=VMEM_LIMIT_BYTES),
        name="relu2_mlp",
    )(*args)


def kernel(x, mem, positions, norm_mix_g, w_in, conv_w, sinks, gnorm_conv_g, gnorm_attn_g, w_out, norm_x_g, norm_mem_g, wx_q, wx_kv, wx_o, norm_mlp_g, w_up, w_down, final_g):
    b, s, d = x.shape
    depth = w_in.shape[0]
    cos, sin, kt, v = _prepare(positions, mem, norm_mem_g, wx_kv)
    for l in range(depth):
        x = _mix_layer(l, x, cos, sin, sinks, norm_mix_g, w_in, conv_w,
                       gnorm_conv_g, gnorm_attn_g, w_out,
                       norm_x_g, wx_q, kt, v, wx_o)
        x = _mlp_layer(l, x.reshape(b * s, d), norm_mlp_g, w_up, w_down,
                       final_g if l == depth - 1 else None).reshape(b, s, d)
    return x
```

```python
import functools
import math

import jax
import jax.numpy as jnp
from jax import lax
from jax.experimental import pallas as pl
from jax.experimental.pallas import tpu as pltpu

D_MODEL = 1024
N_MEM = 256
CONV_CH = 512
CONV_K = 3
HEAD_DIM = 64
N_Q_HEADS = 8
N_KV_HEADS = 2
ATTN_W = N_Q_HEADS * HEAD_DIM
KV_W = N_KV_HEADS * HEAD_DIM
WINDOW = 128
ROPE_THETA = 10000.0
N_X_HEADS = 4
X_HEAD_DIM = D_MODEL // N_X_HEADS
D_FF = 4 * D_MODEL
EPS = 1e-6
LOG2E = math.log2(math.e)
IN_COLS = 3 * CONV_CH + ATTN_W + 2 * KV_W
Q_OFF = 3 * CONV_CH
K_OFF = Q_OFF + ATTN_W

LANES = 128
SUBLANES = 8
V7X_VMEM_BYTES = 64 * 1024 * 1024
VMEM_LIMIT_BYTES = V7X_VMEM_BYTES - 6 * 1024 * 1024

SEQ_TILE = 1024
SUB_TILE = 256
FRONT_ROWS = 512
MLP_STREAM_TILE = 512
FF_CHUNK = 512
CARRY_ROWS = SUBLANES

BF16 = jnp.bfloat16
F32 = jnp.float32


def _rms(x, g):
    ms = jnp.mean(x * x, axis=-1, keepdims=True)
    return x * lax.rsqrt(ms + EPS) * g


def _dot(a, b):
    return jnp.dot(a, b.astype(BF16), preferred_element_type=F32)


def _resident(shape, index_map):
    return pl.BlockSpec(shape, index_map, pipeline_mode=pl.Buffered(1))


def _prep_kernel(pos_ref, inv_ref, mem_ref, g_ref, wkv_ref, cos_ref, sin_ref, kt_ref, v_ref,
                 *, groups_per_layer):
    i = pl.program_id(0)
    pos = pos_ref[pl.ds(i, 1), :]
    ang = inv_ref[...] * pos.astype(F32)
    c = jnp.cos(ang)
    s = jnp.sin(ang)
    cos_ref[0] = jnp.concatenate([c, c, c, c], axis=0).T
    sin_ref[0] = jnp.concatenate([-s, s, -s, s], axis=0).T

    nb, m, d = mem_ref.shape
    g = g_ref[pl.ds(i // groups_per_layer, 1), :]
    memn = _rms(mem_ref[...].reshape(nb * m, d), g).astype(BF16)
    kv = _dot(memn, wkv_ref[...])
    for b in range(nb):
        rows = slice(b * m, (b + 1) * m)
        kt_ref[b] = kv[rows, 0:d].T.astype(BF16)
        v_ref[b] = kv[rows, d:2 * d].astype(BF16)


def _prepare(positions, mem, norm_mem_g, wx_kv):
    b, s = positions.shape
    _, m, d = mem.shape
    depth = wx_kv.shape[0]
    assert b % depth == 0
    groups = b // depth
    half = HEAD_DIM // 2
    inv_freq = ROPE_THETA ** (-jnp.arange(0, HEAD_DIM, 2, dtype=F32) / HEAD_DIM)
    table = pl.BlockSpec((1, s, LANES), lambda i: (i, 0, 0))
    return pl.pallas_call(
        functools.partial(_prep_kernel, groups_per_layer=groups),
        grid=(b,),
        in_specs=[pl.BlockSpec((b, s), lambda i: (0, 0)),
                  pl.BlockSpec((half, 1), lambda i: (0, 0)),
                  pl.BlockSpec((depth, m, d), lambda i: (i % groups, 0, 0)),
                  pl.BlockSpec((depth, d), lambda i: (0, 0)),
                  pl.BlockSpec((None, d, 2 * d), lambda i: (i // groups, 0, 0))],
        out_specs=[table, table,
                   pl.BlockSpec((None, depth, d, m), lambda i: (i // groups, i % groups, 0, 0)),
                   pl.BlockSpec((None, depth, m, d), lambda i: (i // groups, i % groups, 0, 0))],
        out_shape=[jax.ShapeDtypeStruct((b, s, LANES), F32)] * 2
        + [jax.ShapeDtypeStruct((depth, b, d, m), BF16),
           jax.ShapeDtypeStruct((depth, b, m, d), BF16)],
        compiler_params=pltpu.CompilerParams(
            dimension_semantics=("arbitrary",),
            vmem_limit_bytes=VMEM_LIMIT_BYTES),
        name="rope_and_mem_kv",
    )(positions, inv_freq.reshape(half, 1), mem, norm_mem_g, wx_kv)


def _mix_kernel(sinks_ref, x_ref, cos_ref, sin_ref, gmix_ref, win_ref, convw_ref,
                gconv_ref, gattn_ref, wout_ref, gx_ref, wq_ref, kt_ref, v_ref, wo_ref, o_ref,
                ke0, ko0, ke1, ko1, ve0, vo0, ve1, vo1, cbuf, qbuf, abuf, *, layer):
    ts = SEQ_TILE
    kbufs = ((ke0, ko0), (ke1, ko1))
    vbufs = ((ve0, vo0), (ve1, vo1))
    kv_all = (ke0, ko0, ke1, ko1, ve0, vo0, ve1, vo1)
    seq_start = pl.program_id(1) == 0

    @pl.when(seq_start)
    def _():
        for buf in kv_all:
            buf[0:WINDOW, :] = jnp.zeros((WINDOW, LANES), BF16)
        cbuf[0:CARRY_ROWS, :] = jnp.zeros((CARRY_ROWS, CONV_CH), F32)

    half_dim = HEAD_DIM // 2
    scale = LOG2E / math.sqrt(HEAD_DIM)
    w = convw_ref[...]

    qi = lax.broadcasted_iota(jnp.int32, (WINDOW, WINDOW), 0)
    ki = lax.broadcasted_iota(jnp.int32, (WINDOW, WINDOW), 1)
    own = ki <= qi
    no_prev = ki < WINDOW * seq_start.astype(jnp.int32)

    carried = {}

    def front(r0):
        n = FRONT_ROWS
        x = x_ref[0, r0:r0 + n, :]
        h = _rms(x, gmix_ref[layer:layer + 1, :]).astype(BF16)

        cosf = cos_ref[0, r0:r0 + n, :]
        lane = lax.broadcasted_iota(jnp.int32, (n, LANES), 1)
        first_half = (lane & half_dim) == 0
        low_head = lane < HEAD_DIM
        sins = sin_ref[0, r0:r0 + n, :]

        def rope(t):
            rot = jnp.where(first_half,
                            pltpu.roll(t, LANES - half_dim, 1),
                            pltpu.roll(t, half_dim, 1))
            return t * cosf + rot * sins

        uq = _dot(h, win_ref[:, Q_OFF:K_OFF])
        for c in range(ATTN_W // LANES):
            qc = (rope(uq[:, c * LANES:(c + 1) * LANES]) * scale).astype(BF16)
            for jb in range(n // WINDOW):
                qbuf[r0 // WINDOW + jb, c // 2, (c % 2) * WINDOW:(c % 2 + 1) * WINDOW, :] = (
                    qc[jb * WINDOW:(jb + 1) * WINDOW])

        ukv = _dot(h, win_ref[:, K_OFF:IN_COLS])
        k = rope(ukv[:, 0:KV_W])
        v = ukv[:, KV_W:2 * KV_W]
        k0 = WINDOW + r0
        for src, bufs in ((k, kbufs), (v, vbufs)):
            e0 = jnp.where(low_head, src, 0.0)
            o1 = jnp.where(low_head, 0.0, src)
            bufs[0][0][k0:k0 + n, :] = e0.astype(BF16)
            bufs[0][1][k0:k0 + n, :] = pltpu.roll(e0, HEAD_DIM, 1).astype(BF16)
            bufs[1][0][k0:k0 + n, :] = pltpu.roll(o1, HEAD_DIM, 1).astype(BF16)
            bufs[1][1][k0:k0 + n, :] = o1.astype(BF16)

        uc = _dot(h, win_ref[:, 0:Q_OFF])
        cx = uc[:, CONV_CH:2 * CONV_CH] * uc[:, 2 * CONV_CH:3 * CONV_CH]
        c0 = CARRY_ROWS + r0
        cbuf[c0:c0 + n, :] = cx
        y = (w[2:3, :] * cx
             + w[1:2, :] * cbuf[c0 - 1:c0 - 1 + n, :]
             + w[0:1, :] * cbuf[c0 - 2:c0 - 2 + n, :])
        conv_n = _rms(uc[:, 0:CONV_CH] * y, gconv_ref[layer:layer + 1, :]).astype(BF16)
        for c0 in range(0, n, SUB_TILE):
            carried[r0 + c0] = (x[c0:c0 + SUB_TILE], conv_n[c0:c0 + SUB_TILE])

    def back(r0):
        n = SUB_TILE
        x, conv_n = carried.pop(r0)
        for jb in range(n // WINDOW):
            j = r0 // WINDOW + jb
            w0 = j * WINDOW
            for g in range(N_KV_HEADS):
                qg = qbuf[j, g]
                o = None
                for par in range(2):
                    s = lax.dot_general(qg, kbufs[g][par][w0:w0 + 2 * WINDOW, :],
                                        (((1,), (1,)), ((), ())),
                                        preferred_element_type=F32)
                    ps, inv_l = [], []
                    for sub in range(2):
                        sink = sinks_ref[layer, 4 * g + 2 * sub + par] * LOG2E
                        s_prev = s[sub * WINDOW:(sub + 1) * WINDOW, 0:WINDOW]
                        s_own = s[sub * WINDOW:(sub + 1) * WINDOW, WINDOW:2 * WINDOW]
                        if j == 0:
                            s_prev = jnp.where(no_prev, -jnp.inf, s_prev)
                        sh = jnp.where(own, s_own, s_prev)
                        m = jnp.maximum(jnp.max(sh, axis=-1, keepdims=True), sink)
                        p = jnp.exp2(sh - m)
                        l = jnp.sum(p, axis=-1, keepdims=True) + jnp.exp2(sink - m)
                        p_own = jnp.where(own, p, 0.0)
                        ps.append(jnp.concatenate([p - p_own, p_own], axis=1).astype(BF16))
                        inv_l.append(1.0 / l)
                    pv = _dot(jnp.concatenate(ps, axis=0), vbufs[g][par][w0:w0 + 2 * WINDOW, :])
                    o_par = [pv[sub * WINDOW:(sub + 1) * WINDOW] * inv_l[sub] for sub in range(2)]
                    o = o_par if o is None else [a + b for a, b in zip(o, o_par)]
                for sub in range(2):
                    c = 2 * g + sub
                    abuf[j * WINDOW:(j + 1) * WINDOW, c * LANES:(c + 1) * LANES] = o[sub]

        attn_n = _rms(abuf[r0:r0 + n, :], gattn_ref[layer:layer + 1, :]).astype(BF16)
        return (x + _dot(conv_n, wout_ref[0:CONV_CH, :])
                + _dot(attn_n, wout_ref[CONV_CH:CONV_CH + ATTN_W, :]))

    def xattn(r0, x1):
        n = SUB_TILE
        h2 = _rms(x1, gx_ref[layer:layer + 1, :]).astype(BF16)
        xscale = 1.0 / math.sqrt(X_HEAD_DIM)
        q = (_dot(h2, wq_ref[...]) * xscale).astype(BF16)
        outs = []
        for hd in range(N_X_HEADS):
            sl = slice(hd * X_HEAD_DIM, (hd + 1) * X_HEAD_DIM)
            s = _dot(q[:, sl], kt_ref[sl, :])
            m = jnp.max(s, axis=-1, keepdims=True)
            p = jnp.exp(s - m)
            l = jnp.sum(p, axis=-1, keepdims=True)
            outs.append((_dot(p.astype(BF16), v_ref[:, sl]) * (1.0 / l)).astype(BF16))
        o_ref[0, r0:r0 + n, :] = x1 + _dot(jnp.concatenate(outs, axis=1), wo_ref[...])

    starts = list(range(0, ts, SUB_TILE))
    for r0 in range(0, ts, FRONT_ROWS):
        front(r0)
    x1 = {}
    for i, r0 in enumerate(starts):
        x1[i] = back(r0)
        if i >= 1:
            xattn(starts[i - 1], x1.pop(i - 1))
    xattn(starts[-1], x1.pop(len(starts) - 1))

    cbuf[0:CARRY_ROWS, :] = cbuf[ts:ts + CARRY_ROWS, :]
    for buf in kv_all:
        buf[0:WINDOW, :] = buf[ts:ts + WINDOW, :]


def _mix_layer(layer, x, cos, sin, sinks, gmix, w_in, conv_w, gconv, gattn, w_out, gx, wq, kt, v, wo):
    b, s, d = x.shape
    ts = SEQ_TILE
    tok = lambda w: pl.BlockSpec((1, ts, w), lambda i, j: (i, j, 0))
    per_layer = lambda *shape: _resident((None,) + shape, lambda i, j: (layer,) + (0,) * len(shape))
    gains = lambda w: _resident((gmix.shape[0], w), lambda i, j: (0, 0))
    kv_scratch = [pltpu.VMEM((ts + WINDOW, LANES), BF16)] * 8
    return pl.pallas_call(
        functools.partial(_mix_kernel, layer=layer),
        grid=(b, s // ts),
        in_specs=[pl.BlockSpec(memory_space=pltpu.SMEM),
                  tok(d), tok(LANES), tok(LANES),
                  gains(d), per_layer(d, IN_COLS), per_layer(CONV_K, CONV_CH),
                  gains(CONV_CH), gains(ATTN_W), per_layer(CONV_CH + ATTN_W, d),
                  gains(d), per_layer(d, d),
                  pl.BlockSpec((None, None, d, N_MEM), lambda i, j: (layer, i, 0, 0)),
                  pl.BlockSpec((None, None, N_MEM, d), lambda i, j: (layer, i, 0, 0)),
                  per_layer(d, d)],
        out_specs=tok(d),
        out_shape=jax.ShapeDtypeStruct((b, s, d), F32),
        scratch_shapes=kv_scratch + [
            pltpu.VMEM((ts + CARRY_ROWS, CONV_CH), F32),
            pltpu.VMEM((ts // WINDOW, N_KV_HEADS, 2 * WINDOW, LANES), BF16),
            pltpu.VMEM((ts, ATTN_W), F32)],
        compiler_params=pltpu.CompilerParams(
            dimension_semantics=("arbitrary", "arbitrary"),
            vmem_limit_bytes=VMEM_LIMIT_BYTES),
        name="token_mix",
    )(sinks, x, cos, sin, gmix, w_in, conv_w, gconv, gattn, w_out, gx, wq, kt, v, wo)


def _mlp_kernel(x_ref, g_ref, wup_hbm, wdn_hbm, *rest, layer, final):
    fg_ref = rest[0] if final else None
    o_ref, wup_ring, wdn_ring, sem = rest[-4:]
    n_chunks = D_FF // FF_CHUNK
    step = pl.program_id(0)

    def chunk_copies(c):
        slot = c % 2
        cols = pl.ds(c * FF_CHUNK, FF_CHUNK)
        return (pltpu.make_async_copy(wup_hbm.at[layer, :, cols], wup_ring.at[slot], sem.at[0, slot]),
                pltpu.make_async_copy(wdn_hbm.at[layer, cols, :], wdn_ring.at[slot], sem.at[1, slot]))

    def start(c):
        for cp in chunk_copies(c):
            cp.start()

    pl.when(step == 0)(functools.partial(start, 0))
    x = x_ref[...]
    h = _rms(x, g_ref[layer:layer + 1, :]).astype(BF16)
    acc = x
    for c in range(n_chunks):
        if c + 1 < n_chunks:
            start(c + 1)
        else:
            pl.when(step + 1 < pl.num_programs(0))(functools.partial(start, 0))
        for cp in chunk_copies(c):
            cp.wait()
        slot = c % 2
        up = _dot(h, wup_ring[slot])
        act = jnp.square(jnp.maximum(up, 0.0)).astype(BF16)
        acc = acc + _dot(act, wdn_ring[slot])
    if final:
        acc = _rms(acc, fg_ref[...])
    o_ref[...] = acc


def _mlp_layer(layer, x2d, g, w_up, w_down, final_g):
    n, d = x2d.shape
    tm = MLP_STREAM_TILE
    final = final_g is not None
    tok = pl.BlockSpec((tm, d), lambda i: (i, 0))
    in_hbm = pl.BlockSpec(memory_space=pltpu.HBM)
    in_specs = [tok, _resident(g.shape, lambda i: (0, 0)), in_hbm, in_hbm]
    args = [x2d, g, w_up, w_down]
    if final:
        in_specs.append(_resident((1, d), lambda i: (0, 0)))
        args.append(final_g.reshape(1, d))
    return pl.pallas_call(
        functools.partial(_mlp_kernel, layer=layer, final=final),
        grid=(n // tm,),
        in_specs=in_specs,
        out_specs=tok,
        out_shape=jax.ShapeDtypeStruct((n, d), F32),
        scratch_shapes=[pltpu.VMEM((2, d, FF_CHUNK), F32),
                        pltpu.VMEM((2, FF_CHUNK, d), F32),
                        pltpu.SemaphoreType.DMA((2, 2))],
        compiler_params=pltpu.CompilerParams(
            dimension_semantics=("arbitrary",),
            vmem_limit_bytes=VMEM_LIMIT_BYTES),
        name="relu2_mlp",
    )(*args)


def kernel(x, mem, positions, norm_mix_g, w_in, conv_w, sinks, gnorm_conv_g, gnorm_attn_g, w_out, norm_x_g, norm_mem_g, wx_q, wx_kv, wx_o, norm_mlp_g, w_up, w_down, final_g):
    b, s, d = x.shape
    depth = w_in.shape[0]
    cos, sin, kt, v = _prepare(positions, mem, norm_mem_g, wx_kv)
    for l in range(depth):
        x = _mix_layer(l, x, cos, sin, sinks, norm_mix_g, w_in, conv_w,
                       gnorm_conv_g, gnorm_attn_g, w_out,
                       norm_x_g, wx_q, kt, v, wx_o)
        x = _mlp_layer(l, x.reshape(b * s, d), norm_mlp_g, w_up, w_down,
                       final_g if l == depth - 1 else None).reshape(b, s, d)
    return x
```
